```python
import math
import jax, jax.numpy as jnp
from jax import lax
import numpy as np

D_MODEL = 1024
BATCH = 2
SEQ = 8192
DEPTH = 2

N_EVEN = (DEPTH + 1) // 2
N_ODD = DEPTH // 2
MIX_WIDTH = D_MODEL
HEAD_DIM = 64
GMLP_GROUPS = 4
GMLP_CHUNK = 128
GMLP_DIM = MIX_WIDTH // 2 // GMLP_GROUPS
GMLP_WIDTH = GMLP_GROUPS * GMLP_DIM
MOBA_HEADS = (MIX_WIDTH // 2) // HEAD_DIM
MOBA_WIDTH = MOBA_HEADS * HEAD_DIM
MOBA_BLOCK = 256
MOBA_TOPK = 3
MOBA_Q_BLOCK = 64
ROPE_THETA = 500000.0
ROPE_DIM = HEAD_DIM // 4
SB_HEADS = MIX_WIDTH // HEAD_DIM
SB_Q_BLOCK = 128
PEER_HEADS = 8
PEER_NKEYS = 128
PEER_NEXPERTS = PEER_NKEYS * PEER_NKEYS
PEER_DKEY = 256
PEER_TOPK = 16
PEER_TOKEN_CHUNK = 128
EVEN_IN = 2 * GMLP_WIDTH + 3 * MOBA_WIDTH
ODD_IN = 3 * SB_HEADS * HEAD_DIM
EPS = 1e-6

kernel_name = "hybrid_gmlp_moba_stickbreak_peer"


def rmsnorm(x, g):
    xf = x.astype(jnp.float32)
    y = xf * lax.rsqrt(jnp.mean(xf * xf, axis=-1, keepdims=True) + EPS)
    return (y * g.astype(jnp.float32)).astype(x.dtype)


def partial_rope(x, pos):
    half = ROPE_DIM // 2
    inv_freq = ROPE_THETA ** (-jnp.arange(half, dtype=jnp.float32) / half)
    ang = pos.astype(jnp.float32)[:, None] * inv_freq[None, :]
    cos = jnp.cos(ang)[None, :, None, :]
    sin = jnp.sin(ang)[None, :, None, :]
    xr = x[..., :ROPE_DIM].astype(jnp.float32)
    x1, x2 = xr[..., :half], xr[..., half:]
    rot = jnp.concatenate([x1 * cos - x2 * sin, x1 * sin + x2 * cos], axis=-1)
    return jnp.concatenate([rot.astype(x.dtype), x[..., ROPE_DIM:]], axis=-1)


def gmlp_spatial_gating(u, v, v_gain, w_s, b_s):
    B_, S_, G, Dg = u.shape
    u = jax.nn.gelu(u)
    v = rmsnorm(jax.nn.gelu(v), v_gain)
    n_chunks = S_ // GMLP_CHUNK
    vc = v.reshape(B_, n_chunks, GMLP_CHUNK, G, Dg)
    causal = jnp.tril(jnp.ones((GMLP_CHUNK, GMLP_CHUNK), dtype=bool))
    w = jnp.where(causal[None], w_s, 0.0).astype(v.dtype)
    mixed = jnp.einsum('gts,bcsgd->bctgd', w, vc) + b_s.T[None, None, :, :, None].astype(v.dtype)
    return u * mixed.reshape(B_, S_, G, Dg)


def moba_attention(q, k, v):
    B_, H, S_, Dh = q.shape
    scale = Dh ** -0.5
    nb = -(-S_ // MOBA_BLOCK)
    k_sel = min(MOBA_TOPK, nb)
    pad = nb * MOBA_BLOCK - S_
    kp = jnp.pad(k, ((0, 0), (0, 0), (0, pad), (0, 0)))
    vp = jnp.pad(v, ((0, 0), (0, 0), (0, pad), (0, 0)))
    kb = kp.reshape(B_, H, nb, MOBA_BLOCK, Dh)
    vb = vp.reshape(B_, H, nb, MOBA_BLOCK, Dh)
    kmean = jnp.mean(kb.astype(jnp.float32), axis=3)
    nq = S_ // MOBA_Q_BLOCK
    qc = q.reshape(B_, H, nq, MOBA_Q_BLOCK, Dh).transpose(2, 0, 1, 3, 4)
    bidx = jnp.arange(B_)[:, None, None, None]
    hidx = jnp.arange(H)[None, :, None, None]
    n_sel = k_sel * MOBA_BLOCK

    def one_block(args):
        ci, qi = args
        q_pos = ci * MOBA_Q_BLOCK + jnp.arange(MOBA_Q_BLOCK)
        own = (ci * MOBA_Q_BLOCK) // MOBA_BLOCK
        gate = jnp.einsum('bhqd,bhnd->bhqn', qi.astype(jnp.float32), kmean)
        gate = jnp.where(jnp.arange(nb) < own, gate, -jnp.inf)
        _, sel = lax.top_k(gate, k_sel)
        sel_valid = jnp.arange(k_sel) < own
        ks = kb[bidx, hidx, sel]
        vs = vb[bidx, hidx, sel]
        s_sel = jnp.einsum('bhqd,bhqnkd->bhqnk', qi, ks).astype(jnp.float32) * scale
        s_sel = jnp.where(sel_valid[:, None], s_sel, -jnp.inf).reshape(B_, H, MOBA_Q_BLOCK, n_sel)
        k_own = lax.dynamic_index_in_dim(kb, own, axis=2, keepdims=False)
        v_own = lax.dynamic_index_in_dim(vb, own, axis=2, keepdims=False)
        key_pos = own * MOBA_BLOCK + jnp.arange(MOBA_BLOCK)
        s_own = jnp.einsum('bhqd,bhkd->bhqk', qi, k_own).astype(jnp.float32) * scale
        s_own = jnp.where(key_pos[None, :] <= q_pos[:, None], s_own, -jnp.inf)
        p = jax.nn.softmax(jnp.concatenate([s_sel, s_own], axis=-1), axis=-1)
        p_sel = p[..., :n_sel].reshape(B_, H, MOBA_Q_BLOCK, k_sel, MOBA_BLOCK).astype(v.dtype)
        p_own = p[..., n_sel:].astype(v.dtype)
        return (jnp.einsum('bhqnk,bhqnkd->bhqd', p_sel, vs)
                + jnp.einsum('bhqk,bhkd->bhqd', p_own, v_own))

    out = lax.map(one_block, (jnp.arange(nq), qc))
    return out.transpose(1, 2, 0, 3, 4).reshape(B_, H, S_, Dh)


def stick_breaking_attention(q, k, v):
    B_, H, S_, Dh = q.shape
    scale = Dh ** -0.5
    nq = S_ // SB_Q_BLOCK
    qc = q.reshape(B_, H, nq, SB_Q_BLOCK, Dh).transpose(2, 0, 1, 3, 4)
    key_pos = jnp.arange(S_)

    def one_block(args):
        ci, qi = args
        q_pos = ci * SB_Q_BLOCK + jnp.arange(SB_Q_BLOCK)
        causal = key_pos[None, :] < q_pos[:, None]
        z = jnp.einsum('bhqd,bhkd->bhqk', qi, k).astype(jnp.float32) * scale
        log_beta = jax.nn.log_sigmoid(z)
        log_one_minus = jnp.where(causal, jax.nn.log_sigmoid(-z), 0.0)
        log_remaining = lax.cumsum(log_one_minus, axis=3, reverse=True) - log_one_minus
        a = jnp.where(causal, jnp.exp(log_beta + log_remaining), 0.0)
        return jnp.einsum('bhqk,bhkd->bhqd', a.astype(v.dtype), v)

    out = lax.map(one_block, (jnp.arange(nq), qc))
    return out.transpose(1, 2, 0, 3, 4).reshape(B_, H, S_, Dh)


def peer_ffn(h, w_q, subkeys, u_tab, v_tab):
    B_, S_, D = h.shape
    q = jnp.einsum('bsd,de->bse', h, w_q).reshape(B_, S_, PEER_HEADS, 2, PEER_DKEY // 2)
    s = jnp.einsum('bshpd,hpnd->bshpn', q, subkeys).astype(jnp.float32)
    s_top, i_top = lax.top_k(s, PEER_TOPK)
    cand = s_top[..., 0, :, None] + s_top[..., 1, None, :]
    cand_idx = i_top[..., 0, :, None] * PEER_NKEYS + i_top[..., 1, None, :]
    n_cand = PEER_TOPK * PEER_TOPK
    best, pos = lax.top_k(cand.reshape(B_, S_, PEER_HEADS, n_cand), PEER_TOPK)
    experts = jnp.take_along_axis(cand_idx.reshape(B_, S_, PEER_HEADS, n_cand), pos, axis=-1)
    gates = jax.nn.softmax(best, axis=-1).astype(h.dtype)
    n_tok = B_ * S_
    nt = n_tok // PEER_TOKEN_CHUNK
    hc = h.reshape(nt, PEER_TOKEN_CHUNK, D)
    ec = experts.reshape(nt, PEER_TOKEN_CHUNK, PEER_HEADS * PEER_TOPK)
    gc = gates.reshape(nt, PEER_TOKEN_CHUNK, PEER_HEADS * PEER_TOPK)

    def one_chunk(args):
        hi, ei, gi = args
        u = u_tab[ei]
        act = jax.nn.gelu(jnp.einsum('td,ted->te', hi, u))
        vv = v_tab[ei]
        return jnp.einsum('te,ted->td', gi * act, vv)

    out = lax.map(one_chunk, (hc, ec, gc))
    return out.reshape(B_, S_, D)


def setup_inputs(seed: int = 0) -> dict:
    key = jax.random.key(seed)
    ks = jax.random.split(key, 20)
    nrm = jax.random.normal
    f32 = jnp.float32
    D = D_MODEL
    return {
        "x": nrm(ks[0], (BATCH, SEQ, D), f32),
        "c": nrm(ks[1], (BATCH, D), f32),
        "w_mod": nrm(ks[2], (DEPTH, D, 6 * D), f32) * (0.5 * D ** -0.5),
        "b_mod": nrm(ks[3], (DEPTH, 6 * D), f32) * 0.02,
        "norm_mix": 1.0 + 0.02 * nrm(ks[4], (DEPTH, D), f32),
        "norm_ffn": 1.0 + 0.02 * nrm(ks[5], (DEPTH, D), f32),
        "even_w_in": nrm(ks[6], (N_EVEN, D, EVEN_IN), f32) * D ** -0.5,
        "gmlp_v_gain": 1.0 + 0.02 * nrm(ks[7], (N_EVEN, GMLP_GROUPS, GMLP_DIM), f32),
        "gmlp_w_s": nrm(ks[8], (N_EVEN, GMLP_GROUPS, GMLP_CHUNK, GMLP_CHUNK), f32) * GMLP_CHUNK ** -0.5,
        "gmlp_b_s": 1.0 + 0.02 * nrm(ks[9], (N_EVEN, GMLP_GROUPS, GMLP_CHUNK), f32),
        "moba_q_gain": 1.0 + 0.02 * nrm(ks[10], (N_EVEN, HEAD_DIM), f32),
        "moba_k_gain": 1.0 + 0.02 * nrm(ks[11], (N_EVEN, HEAD_DIM), f32),
        "even_w_out": nrm(ks[12], (N_EVEN, MIX_WIDTH, D), f32) * MIX_WIDTH ** -0.5,
        "odd_w_in": nrm(ks[13], (N_ODD, D, ODD_IN), f32) * D ** -0.5,
        "odd_w_out": nrm(ks[14], (N_ODD, MIX_WIDTH, D), f32) * MIX_WIDTH ** -0.5,
        "peer_w_q": nrm(ks[15], (DEPTH, D, PEER_HEADS * PEER_DKEY), f32) * D ** -0.5,
        "peer_subkeys": nrm(ks[16], (DEPTH, PEER_HEADS, 2, PEER_NKEYS, PEER_DKEY // 2), f32) * (PEER_DKEY // 2) ** -0.5,
        "peer_u": nrm(ks[17], (DEPTH, PEER_NEXPERTS, D), f32) * D ** -0.5,
        "peer_v": nrm(ks[18], (DEPTH, PEER_NEXPERTS, D), f32),
    }


def reference(x, c, w_mod, b_mod, norm_mix, norm_ffn, even_w_in, gmlp_v_gain, gmlp_w_s,
              gmlp_b_s, moba_q_gain, moba_k_gain, even_w_out, odd_w_in, odd_w_out,
              peer_w_q, peer_subkeys, peer_u, peer_v):
    B_, S_, D = x.shape
    pos = jnp.arange(S_)
    c_act = jax.nn.silu(c)
    split_even = [GMLP_WIDTH, 2 * GMLP_WIDTH, 2 * GMLP_WIDTH + MOBA_WIDTH,
                  2 * GMLP_WIDTH + 2 * MOBA_WIDTH]
    for layer in range(DEPTH):
        i = layer // 2
        mod = (c_act @ w_mod[layer] + b_mod[layer])[:, None, :]
        sh1, sc1, g1, sh2, sc2, g2 = jnp.split(mod, 6, axis=-1)
        h = rmsnorm(x, norm_mix[layer]) * (1.0 + sc1) + sh1
        if layer % 2 == 0:
            proj = h @ even_w_in[i]
            ua, va, qb, kb, vb = jnp.split(proj, split_even, axis=-1)
            ya = gmlp_spatial_gating(ua.reshape(B_, S_, GMLP_GROUPS, GMLP_DIM),
                                     va.reshape(B_, S_, GMLP_GROUPS, GMLP_DIM),
                                     gmlp_v_gain[i], gmlp_w_s[i], gmlp_b_s[i])
            q = partial_rope(rmsnorm(qb.reshape(B_, S_, MOBA_HEADS, HEAD_DIM), moba_q_gain[i]), pos)
            k = partial_rope(rmsnorm(kb.reshape(B_, S_, MOBA_HEADS, HEAD_DIM), moba_k_gain[i]), pos)
            v = vb.reshape(B_, S_, MOBA_HEADS, HEAD_DIM)
            yb = moba_attention(q.transpose(0, 2, 1, 3), k.transpose(0, 2, 1, 3), v.transpose(0, 2, 1, 3))
            y = jnp.concatenate([ya.reshape(B_, S_, GMLP_WIDTH),
                                 yb.transpose(0, 2, 1, 3).reshape(B_, S_, MOBA_WIDTH)], axis=-1) @ even_w_out[i]
        else:
            q, k, v = jnp.split(h @ odd_w_in[i], 3, axis=-1)
            q = q.reshape(B_, S_, SB_HEADS, HEAD_DIM).transpose(0, 2, 1, 3)
            k = k.reshape(B_, S_, SB_HEADS, HEAD_DIM).transpose(0, 2, 1, 3)
            v = v.reshape(B_, S_, SB_HEADS, HEAD_DIM).transpose(0, 2, 1, 3)
            ys = stick_breaking_attention(q, k, v)
            y = ys.transpose(0, 2, 1, 3).reshape(B_, S_, MIX_WIDTH) @ odd_w_out[i]
        x = x + g1 * y
        h = rmsnorm(x, norm_ffn[layer]) * (1.0 + sc2) + sh2
        x = x + g2 * peer_ffn(h, peer_w_q[layer], peer_subkeys[layer], peer_u[layer], peer_v[layer])
    return x
```

```python
import functools

import jax
import jax.numpy as jnp
from jax import lax
from jax.experimental import pallas as pl
from jax.experimental.pallas import tpu as pltpu

F32 = jnp.float32
BF16 = jnp.bfloat16
HIGHEST = lax.Precision.HIGHEST

EPS = 1e-6
HEAD_DIM = 64
GMLP_GROUPS = 4
GMLP_CHUNK = 128
MOBA_BLOCK = 256
MOBA_TOPK = 3
ROPE_THETA = 500000.0
ROPE_DIM = HEAD_DIM // 4
PEER_HEADS = 8
PEER_NKEYS = 128
PEER_TOPK = 16
NEG_BIG = -1e30

VMEM_LIMIT_BYTES = 56 * 1024 * 1024
TOKEN_TILE = 512
ATTN_TILE = 256
PEER_TOKEN_TILE = 512
PEER_EXPERT_TILE = 256
NT_DIMS = (((1,), (1,)), ((), ()))


def _cparams(*sem):
    return pltpu.CompilerParams(dimension_semantics=sem, vmem_limit_bytes=VMEM_LIMIT_BYTES)


def _dot(a, b, **kw):
    return jnp.dot(a, b, preferred_element_type=F32, **kw)


def _group_sum(x, blockdiag):
    hi = x.astype(BF16)
    lo = (x - hi.astype(F32)).astype(BF16)
    return _dot(hi, blockdiag) + _dot(lo, blockdiag)


def _adaln(x, gain, sc, sh):
    ms = jnp.mean(x * x, axis=-1, keepdims=True)
    return x * lax.rsqrt(ms + EPS) * gain * (1.0 + sc) + sh


def _mod_kernel(c_ref, w_ref, b_ref, o_ref):
    c = c_ref[...]
    ca = c * jax.nn.sigmoid(c)
    o_ref[0] = _dot(ca, w_ref[0], precision=HIGHEST) + b_ref[0]


def _modulation(c, w_mod, b_mod):
    depth, d, d6 = w_mod.shape
    bsz = c.shape[0]
    rows = 8
    c_pad = jnp.zeros((rows, d), F32).at[:bsz].set(c)
    tn = 1536
    out = pl.pallas_call(
        _mod_kernel,
        grid=(depth, d6 // tn),
        in_specs=[
            pl.BlockSpec((rows, d), lambda l, n: (0, 0)),
            pl.BlockSpec((1, d, tn), lambda l, n: (l, 0, n)),
            pl.BlockSpec((1, 1, tn), lambda l, n: (l, 0, n)),
        ],
        out_specs=pl.BlockSpec((1, rows, tn), lambda l, n: (l, 0, n)),
        out_shape=jax.ShapeDtypeStruct((depth, rows, d6), F32),
        compiler_params=_cparams("arbitrary", "arbitrary"),
        name="adaln_modulation",
    )(c_pad, w_mod, b_mod.reshape(depth, 1, d6))
    return out[:, :bsz]


def _in0_kernel(x_ref, sc_ref, sh_ref, gn_ref, w_ref, vgain_ref, ws_ref, bs_ref, qg_ref, kg_ref,
                cos_ref, sina_ref, sinb_ref, bd128_ref, bd64_ref,
                ya_ref, q_ref, k_ref, v_ref, km_ref):
    tm = x_ref.shape[0]
    gw = GMLP_GROUPS * GMLP_CHUNK
    h = _adaln(x_ref[...], gn_ref[...], sc_ref[0], sh_ref[0])
    proj = _dot(h.astype(BF16), w_ref[...])
    ua, va = proj[:, :gw], proj[:, gw:2 * gw]
    qb, kb, vb = proj[:, 2 * gw:3 * gw], proj[:, 3 * gw:4 * gw], proj[:, 4 * gw:5 * gw]

    u = jax.nn.gelu(ua)
    gv = jax.nn.gelu(va)
    ss = _group_sum(gv * gv, bd128_ref[...]) * (1.0 / GMLP_CHUNK)
    vn = (gv * lax.rsqrt(ss + EPS) * vgain_ref[...]).astype(BF16)
    r = lax.broadcasted_iota(jnp.int32, (GMLP_CHUNK, GMLP_CHUNK), 0)
    c = lax.broadcasted_iota(jnp.int32, (GMLP_CHUNK, GMLP_CHUNK), 1)
    for g in range(GMLP_GROUPS):
        wg = jnp.where(c <= r, ws_ref[g], 0.0).astype(BF16)
        lanes = slice(g * GMLP_CHUNK, (g + 1) * GMLP_CHUNK)
        for ch in range(tm // GMLP_CHUNK):
            rows = slice(ch * GMLP_CHUNK, (ch + 1) * GMLP_CHUNK)
            mixed = _dot(wg, vn[rows, lanes]) + bs_ref[g]
            ya_ref[rows, lanes] = (u[rows, lanes] * mixed).astype(BF16)

    reps = gw // cos_ref.shape[1]
    cos = jnp.concatenate([cos_ref[...]] * reps, axis=1)
    sina = jnp.concatenate([sina_ref[...]] * reps, axis=1)
    sinb = jnp.concatenate([sinb_ref[...]] * reps, axis=1)
    half = ROPE_DIM // 2

    def norm_rope(t, gain):
        ms = _group_sum(t * t, bd64_ref[...]) * (1.0 / HEAD_DIM)
        tn = t * lax.rsqrt(ms + EPS) * gain
        return tn * cos + pltpu.roll(tn, gw - half, 1) * sina + pltpu.roll(tn, half, 1) * sinb

    q = norm_rope(qb, qg_ref[...]) * (HEAD_DIM ** -0.5)
    k = norm_rope(kb, kg_ref[...])
    q_ref[...] = q
    k_ref[...] = k.astype(BF16)
    v_ref[...] = vb.astype(BF16)
    km_ref[0] = jnp.mean(k.reshape(tm // MOBA_BLOCK, MOBA_BLOCK, gw), axis=1)


def _rope_tables(seq):
    half = ROPE_DIM // 2
    inv_freq = ROPE_THETA ** (-jnp.arange(half, dtype=F32) / half)
    ang = jnp.arange(seq, dtype=F32)[:, None] * inv_freq[None, :]
    lane = jnp.arange(2 * HEAD_DIM) % HEAD_DIM
    cos_l = jnp.cos(ang)[:, lane % half]
    sin_l = jnp.sin(ang)[:, lane % half]
    cos = jnp.where(lane < ROPE_DIM, cos_l, 1.0)
    sina = jnp.where(lane < half, -sin_l, 0.0)
    sinb = jnp.where((lane >= half) & (lane < ROPE_DIM), sin_l, 0.0)
    return cos, sina, sinb


def _blockdiag(n, group):
    idx = jnp.arange(n) // group
    return (idx[:, None] == idx[None, :]).astype(BF16)


def _in_proj0(x2d, seq, sc, sh, gain, w_in, v_gain, w_s, b_s, q_gain, k_gain):
    n, d = x2d.shape
    tm = TOKEN_TILE
    tiles_per_seq = seq // tm
    gw = GMLP_GROUPS * GMLP_CHUNK
    n_in = w_in.shape[1]
    cos, sina, sinb = _rope_tables(seq)
    heads = gw // HEAD_DIM
    row_spec = lambda w: pl.BlockSpec((tm, w), lambda i: (i, 0))
    const2 = lambda a: pl.BlockSpec(a.shape, lambda i: (0,) * a.ndim)
    mod_spec = pl.BlockSpec((1, 1, d), lambda i: (i // tiles_per_seq, 0, 0))
    rope_spec = pl.BlockSpec((tm, 2 * HEAD_DIM), lambda i: (i % tiles_per_seq, 0))
    args = [x2d, sc, sh, gain.reshape(1, d), w_in.astype(BF16), v_gain.reshape(1, gw), w_s,
            jnp.broadcast_to(b_s[:, :, None], b_s.shape + (GMLP_CHUNK,)),
            jnp.tile(q_gain, heads).reshape(1, gw), jnp.tile(k_gain, heads).reshape(1, gw),
            cos, sina, sinb, _blockdiag(gw, GMLP_CHUNK), _blockdiag(gw, HEAD_DIM)]
    in_specs = [row_spec(d), mod_spec, mod_spec] + [const2(a) for a in args[3:10]] \
        + [rope_spec] * 3 + [const2(a) for a in args[13:]]
    ya, q, k, v, km = pl.pallas_call(
        _in0_kernel,
        grid=(n // tm,),
        in_specs=in_specs,
        out_specs=[row_spec(gw), row_spec(gw), row_spec(gw), row_spec(gw),
                   pl.BlockSpec((1, tm // MOBA_BLOCK, gw), lambda i: (i, 0, 0))],
        out_shape=[jax.ShapeDtypeStruct((n, gw), BF16), jax.ShapeDtypeStruct((n, gw), F32),
                   jax.ShapeDtypeStruct((n, gw), BF16), jax.ShapeDtypeStruct((n, gw), BF16),
                   jax.ShapeDtypeStruct((n // tm, tm // MOBA_BLOCK, gw), F32)],
        compiler_params=_cparams("arbitrary"),
        name="layer0_in_proj",
    )(*args)
    assert n_in == 5 * gw
    return ya, q, k, v, km.reshape(n // MOBA_BLOCK, gw)


def _moba_kernel(q_ref, k_ref, v_ref, km_ref, o_ref):
    i = pl.program_id(2)
    t = ATTN_TILE
    nb = km_ref.shape[2]
    q32 = q_ref[0, 0]
    q = q32.astype(BF16)
    gate = lax.dot_general(q32, km_ref[0, 0], NT_DIMS, precision=HIGHEST, preferred_element_type=F32)
    blk = lax.broadcasted_iota(jnp.int32, (t, nb), 1)
    past = blk < i
    g = jnp.where(past, gate, -jnp.inf)
    kth = None
    for _ in range(MOBA_TOPK):
        kth = jnp.max(g, axis=1, keepdims=True)
        g = jnp.where(g >= kth, -jnp.inf, g)
    sel = jnp.where(past & (gate >= kth), 1.0, 0.0)
    row = lax.broadcasted_iota(jnp.int32, (t, t), 0)
    col = lax.broadcasted_iota(jnp.int32, (t, t), 1)

    def update(jb, keep, carry):
        m_old, l_old, acc = carry
        start = pl.multiple_of(jb * t, t)
        kb = k_ref[0, 0, pl.ds(start, t), :]
        vb = v_ref[0, 0, pl.ds(start, t), :]
        s = lax.dot_general(q, kb, NT_DIMS, preferred_element_type=F32)
        s = jnp.where(keep, s, NEG_BIG)
        m_new = jnp.maximum(m_old, jnp.max(s, axis=1, keepdims=True))
        alpha = jnp.exp(m_old - m_new)
        p = jnp.exp(s - m_new)
        l_new = alpha * l_old + jnp.sum(p, axis=1, keepdims=True)
        acc = alpha * acc + _dot(p.astype(BF16), vb)
        return m_new, l_new, acc

    def past_block(jb, carry):
        picked = jnp.sum(jnp.where(blk == jb, sel, 0.0), axis=1, keepdims=True)
        return update(jb, picked > 0.5, carry)

    init = (jnp.full((t, 1), NEG_BIG, F32), jnp.zeros((t, 1), F32), jnp.zeros((t, HEAD_DIM), F32))
    carry = lax.fori_loop(0, i, past_block, init)
    _, l_fin, acc = update(i, col <= row, carry)
    o_ref[0, 0] = (acc / l_fin).astype(o_ref.dtype)


def _moba(q, k, v, kmean):
    b, h, s, dh = q.shape
    t = ATTN_TILE
    nb = s // MOBA_BLOCK
    assert t == MOBA_BLOCK
    tile = pl.BlockSpec((1, 1, t, dh), lambda bi, hi, i: (bi, hi, i, 0))
    full = pl.BlockSpec((1, 1, s, dh), lambda bi, hi, i: (bi, hi, 0, 0))
    return pl.pallas_call(
        _moba_kernel,
        grid=(b, h, s // t),
        in_specs=[tile, full, full, pl.BlockSpec((1, 1, nb, dh), lambda bi, hi, i: (bi, hi, 0, 0))],
        out_specs=tile,
        out_shape=jax.ShapeDtypeStruct((b, h, s, dh), BF16),
        compiler_params=_cparams("arbitrary", "arbitrary", "arbitrary"),
        name="moba_attention",
    )(q, k, v, kmean)


def _sb_kernel(q_ref, k_ref, v_ref, tri_ref, o_ref):
    i = pl.program_id(2)
    t = ATTN_TILE
    q = q_ref[0, 0]
    tri = tri_ref[...]
    row = lax.broadcasted_iota(jnp.int32, (t, t), 0)
    col = lax.broadcasted_iota(jnp.int32, (t, t), 1)
    strictly_past = col < row

    def block(jb, diag, carry):
        run, acc = carry
        start = pl.multiple_of(jb * t, t)
        kb = k_ref[0, 0, pl.ds(start, t), :]
        vb = v_ref[0, 0, pl.ds(start, t), :]
        z = lax.dot_general(q, kb, NT_DIMS, preferred_element_type=F32)
        lom = -(jnp.maximum(z, 0.0) + jnp.log(1.0 + jnp.exp(-jnp.abs(z))))
        if diag:
            lom = jnp.where(strictly_past, lom, 0.0)
        hi = lom.astype(BF16)
        lo = (lom - hi.astype(F32)).astype(BF16)
        suffix = _dot(hi, tri) + _dot(lo, tri)
        a = jnp.exp(z + suffix + run)
        if diag:
            a = jnp.where(strictly_past, a, 0.0)
        acc = acc + _dot(a.astype(BF16), vb)
        return run + suffix[:, 0:1], acc

    carry = block(i, True, (jnp.zeros((t, 1), F32), jnp.zeros((t, HEAD_DIM), F32)))
    _, acc = lax.fori_loop(0, i, lambda n, c: block(i - 1 - n, False, c), carry)
    o_ref[0, 0] = acc.astype(o_ref.dtype)


def _stick_breaking(q, k, v):
    b, h, s, dh = q.shape
    t = ATTN_TILE
    idx = jnp.arange(t)
    tri = (idx[:, None] >= idx[None, :]).astype(BF16)
    tile = pl.BlockSpec((1, 1, t, dh), lambda bi, hi, i: (bi, hi, i, 0))
    full = pl.BlockSpec((1, 1, s, dh), lambda bi, hi, i: (bi, hi, 0, 0))
    return pl.pallas_call(
        _sb_kernel,
        grid=(b, h, s // t),
        in_specs=[tile, full, full, pl.BlockSpec((t, t), lambda bi, hi, i: (0, 0))],
        out_specs=tile,
        out_shape=jax.ShapeDtypeStruct((b, h, s, dh), BF16),
        compiler_params=_cparams("arbitrary", "arbitrary", "arbitrary"),
        name="stick_breaking_attention",
    )(q, k, v, tri)


def _in1_kernel(x_ref, sc_ref, sh_ref, gn_ref, w_ref, q_ref, k_ref, v_ref):
    width = q_ref.shape[1]
    h = _adaln(x_ref[...], gn_ref[...], sc_ref[0], sh_ref[0])
    proj = _dot(h.astype(BF16), w_ref[...])
    q_ref[...] = (proj[:, :width] * (HEAD_DIM ** -0.5)).astype(BF16)
    k_ref[...] = proj[:, width:2 * width].astype(BF16)
    v_ref[...] = proj[:, 2 * width:].astype(BF16)


def _in_proj1(x2d, seq, sc, sh, gain, w_in):
    n, d = x2d.shape
    tm = TOKEN_TILE
    tiles_per_seq = seq // tm
    width = w_in.shape[1] // 3
    row_spec = lambda w: pl.BlockSpec((tm, w), lambda i: (i, 0))
    mod_spec = pl.BlockSpec((1, 1, d), lambda i: (i // tiles_per_seq, 0, 0))
    return pl.pallas_call(
        _in1_kernel,
        grid=(n // tm,),
        in_specs=[row_spec(d), mod_spec, mod_spec, pl.BlockSpec((1, d), lambda i: (0, 0)),
                  pl.BlockSpec(w_in.shape, lambda i: (0, 0))],
        out_specs=[row_spec(width)] * 3,
        out_shape=[jax.ShapeDtypeStruct((n, width), BF16)] * 3,
        compiler_params=_cparams("arbitrary"),
        name="layer1_in_proj",
    )(x2d, sc, sh, gain.reshape(1, d), w_in.astype(BF16))


def _out_kernel(*refs, widths):
    n_in = len(widths)
    y_refs = refs[:n_in]
    w_ref, x_ref, g1_ref, gn_ref, sc_ref, sh_ref, x1_ref, h2t_ref = refs[n_in:]
    y = None
    off = 0
    for y_ref, wd in zip(y_refs, widths):
        part = _dot(y_ref[...], w_ref[off:off + wd, :])
        y = part if y is None else y + part
        off += wd
    x1 = x_ref[...] + g1_ref[0] * y
    x1_ref[...] = x1
    h2 = _adaln(x1, gn_ref[...], sc_ref[0], sh_ref[0])
    h2t_ref[...] = h2.T.astype(BF16)


def _out_proj(ys, w_out, x2d, seq, g1, gain, sc, sh):
    n, d = x2d.shape
    tm = TOKEN_TILE
    tiles_per_seq = seq // tm
    widths = tuple(y.shape[1] for y in ys)
    row_spec = lambda w: pl.BlockSpec((tm, w), lambda i: (i, 0))
    mod_spec = pl.BlockSpec((1, 1, d), lambda i: (i // tiles_per_seq, 0, 0))
    return pl.pallas_call(
        functools.partial(_out_kernel, widths=widths),
        grid=(n // tm,),
        in_specs=[row_spec(w) for w in widths]
        + [pl.BlockSpec(w_out.shape, lambda i: (0, 0)), row_spec(d), mod_spec,
           pl.BlockSpec((1, d), lambda i: (0, 0)), mod_spec, mod_spec],
        out_specs=[row_spec(d), pl.BlockSpec((d, tm), lambda i: (0, i))],
        out_shape=[jax.ShapeDtypeStruct((n, d), F32), jax.ShapeDtypeStruct((d, n), BF16)],
        compiler_params=_cparams("arbitrary"),
        name="out_proj_residual",
    )(*ys, w_out.astype(BF16), x2d, g1, gain.reshape(1, d), sc, sh)


def _top_values(s, n):
    vals = []
    for _ in range(n):
        m = jnp.max(s, axis=0, keepdims=True)
        vals.append(m)
        s = jnp.where(s >= m, -jnp.inf, s)
    return vals


def _peer_kernel(h2t_ref, wqt_ref, sk_ref, u_ref, vt_ref, x1_ref, g2_ref, o_ref,
                 s2_sc, p2_sc, thr_sc, c1_sc, acc_sc):
    j = pl.program_id(1)
    n_steps = pl.num_programs(1)
    nk = PEER_NKEYS
    kk = PEER_TOPK

    @pl.when(j == 0)
    def _route():
        qt = _dot(wqt_ref[...], h2t_ref[...])
        for h in range(PEER_HEADS):
            scores, tops = [], []
            for p in range(2):
                qhp = qt[(2 * h + p) * nk:(2 * h + p + 1) * nk, :]
                s = _dot(sk_ref[h, p], qhp, precision=HIGHEST)
                scores.append(s)
                tops.append(_top_values(s, kk + 1))
            a1, a2 = tops
            cand = jnp.concatenate(
                [a1[r1] + a2[r2] for r1 in range(kk + 1) for r2 in range(kk + 1)
                 if (r1 + 1) * (r2 + 1) <= kk + 1], axis=0)
            best = _top_values(cand, kk + 1)
            tau = 0.5 * (best[kk - 1] + best[kk])
            z = jnp.sum(jnp.where(cand >= tau, jnp.exp(cand - best[0]), 0.0), axis=0, keepdims=True)
            s1, s2 = scores
            e1 = jnp.where(s1 >= a1[kk - 1], jnp.exp(s1 - a1[0]), 0.0)
            e2 = jnp.where(s2 >= a2[kk - 1], jnp.exp(s2 - a2[0]), 0.0)
            s2_sc[h] = s2
            p2_sc[h] = e2
            thr_sc[h] = tau - s1
            c1_sc[h] = e1 / z
        acc_sc[...] = jnp.zeros_like(acc_sc)

    te = u_ref.shape[0]
    at = _dot(u_ref[...], h2t_ref[...])
    act = jax.nn.gelu(at)
    parts = []
    for ai in range(te // nk):
        a = j * (te // nk) + ai
        w = None
        for h in range(PEER_HEADS):
            thr = thr_sc[h, pl.ds(a, 1), :]
            c1 = c1_sc[h, pl.ds(a, 1), :]
            term = jnp.where(s2_sc[h] >= thr, p2_sc[h] * c1, 0.0)
            w = term if w is None else w + term
        parts.append(w)
    wmat = jnp.concatenate(parts, axis=0)
    acc_sc[...] += _dot(vt_ref[...], (act * wmat).astype(BF16))

    @pl.when(j == n_steps - 1)
    def _finish():
        o_ref[...] = x1_ref[...] + g2_ref[0] * acc_sc[...].T


def _peer(h2t, x1, seq, g2, w_q, subkeys, u_tab, v_tab):
    d, n = h2t.shape
    tm = PEER_TOKEN_TILE
    te = PEER_EXPERT_TILE
    tiles_per_seq = seq // tm
    ne = u_tab.shape[0]
    heads = subkeys.shape[0]
    scratch = [pltpu.VMEM((heads, PEER_NKEYS, tm), F32) for _ in range(4)] + [pltpu.VMEM((d, tm), F32)]
    return pl.pallas_call(
        _peer_kernel,
        grid=(n // tm, ne // te),
        in_specs=[
            pl.BlockSpec((d, tm), lambda i, j: (0, i)),
            pl.BlockSpec((w_q.shape[1], d), lambda i, j: (0, 0)),
            pl.BlockSpec(subkeys.shape, lambda i, j: (0, 0, 0, 0)),
            pl.BlockSpec((te, d), lambda i, j: (j, 0)),
            pl.BlockSpec((d, te), lambda i, j: (0, j)),
            pl.BlockSpec((tm, d), lambda i, j: (i, 0)),
            pl.BlockSpec((1, 1, d), lambda i, j: (i // tiles_per_seq, 0, 0)),
        ],
        out_specs=pl.BlockSpec((tm, d), lambda i, j: (i, 0)),
        out_shape=jax.ShapeDtypeStruct((n, d), F32),
        scratch_shapes=scratch,
        compiler_params=_cparams("arbitrary", "arbitrary"),
        name="peer_ffn",
    )(h2t, w_q.T.astype(BF16), subkeys, u_tab.astype(BF16), v_tab.T.astype(BF16), x1, g2)


def _split_heads(t, bsz, seq):
    return t.reshape(bsz, seq, -1, HEAD_DIM).transpose(0, 2, 1, 3)


def _merge_heads(t):
    bsz, heads, seq, dh = t.shape
    return t.transpose(0, 2, 1, 3).reshape(bsz * seq, heads * dh)


def kernel(x, c, w_mod, b_mod, norm_mix, norm_ffn, even_w_in, gmlp_v_gain, gmlp_w_s, gmlp_b_s,
           moba_q_gain, moba_k_gain, even_w_out, odd_w_in, odd_w_out, peer_w_q, peer_subkeys,
           peer_u, peer_v):
    bsz, seq, d = x.shape
    depth = w_mod.shape[0]
    assert seq % TOKEN_TILE == 0 and seq % ATTN_TILE == 0 and seq % PEER_TOKEN_TILE == 0
    mod = _modulation(c, w_mod, b_mod)
    xs = x.reshape(bsz * seq, d)
    for layer in range(depth):
        i = layer // 2
        sh1, sc1, g1, sh2, sc2, g2 = [mod[layer, :, k * d:(k + 1) * d].reshape(bsz, 1, d) for k in range(6)]
        if layer % 2 == 0:
            ya, q, k, v, km = _in_proj0(xs, seq, sc1, sh1, norm_mix[layer], even_w_in[i],
                                        gmlp_v_gain[i], gmlp_w_s[i], gmlp_b_s[i],
                                        moba_q_gain[i], moba_k_gain[i])
            yb = _moba(_split_heads(q, bsz, seq), _split_heads(k, bsz, seq), _split_heads(v, bsz, seq),
                       _split_heads(km, bsz, seq // MOBA_BLOCK))
            ys, w_out = [ya, _merge_heads(yb)], even_w_out[i]
        else:
            q, k, v = _in_proj1(xs, seq, sc1, sh1, norm_mix[layer], odd_w_in[i])
            y = _stick_breaking(_split_heads(q, bsz, seq), _split_heads(k, bsz, seq),
                                _split_heads(v, bsz, seq))
            ys, w_out = [_merge_heads(y)], odd_w_out[i]
        x1, h2t = _out_proj(ys, w_out, xs, seq, g1, norm_ffn[layer], sc2, sh2)
        xs = _peer(h2t, x1, seq, g2, peer_w_q[layer], peer_subkeys[layer], peer_u[layer], peer_v[layer])
    return xs.reshape(bsz, seq, d)
```

```python
import functools

import jax
import jax.numpy as jnp
from jax import lax
from jax.experimental import pallas as pl
from jax.experimental.pallas import tpu as pltpu

F32 = jnp.float32
BF16 = jnp.bfloat16
HIGHEST = lax.Precision.HIGHEST

EPS = 1e-6
HEAD_DIM = 64
GMLP_GROUPS = 4
GMLP_CHUNK = 128
MOBA_BLOCK = 256
MOBA_TOPK = 3
ROPE_THETA = 500000.0
ROPE_DIM = HEAD_DIM // 4
PEER_HEADS = 8
PEER_NKEYS = 128
PEER_TOPK = 16
NEG_BIG = -1e30

VMEM_LIMIT_BYTES = 56 * 1024 * 1024
TOKEN_TILE = 512
ATTN_TILE = 256
HEAD_GROUP = 4
SB_LOG_FLOOR = -110.0
PEER_TOKEN_TILE = 512
PEER_EXPERT_TILE = 512
PEER_EXPERT_SUBTILE = 256
NT_DIMS = (((1,), (1,)), ((), ()))


def _cparams(*sem):
    return pltpu.CompilerParams(dimension_semantics=sem, vmem_limit_bytes=VMEM_LIMIT_BYTES)


def _dot(a, b, **kw):
    return jnp.dot(a, b, preferred_element_type=F32, **kw)


def _group_sum(x, blockdiag):
    hi = x.astype(BF16)
    lo = (x - hi.astype(F32)).astype(BF16)
    return _dot(hi, blockdiag) + _dot(lo, blockdiag)


def _adaln(x, gain, sc, sh):
    ms = jnp.mean(x * x, axis=-1, keepdims=True)
    return x * lax.rsqrt(ms + EPS) * gain * (1.0 + sc) + sh


def _mod_kernel(c_ref, w_ref, b_ref, o_ref):
    c = c_ref[...]
    ca = c * jax.nn.sigmoid(c)
    o_ref[0] = _dot(ca, w_ref[0], precision=HIGHEST) + b_ref[0]


def _modulation(c, w_mod, b_mod):
    depth, d, d6 = w_mod.shape
    bsz = c.shape[0]
    rows = 8
    c_pad = jnp.zeros((rows, d), F32).at[:bsz].set(c)
    tn = 1536
    out = pl.pallas_call(
        _mod_kernel,
        grid=(depth, d6 // tn),
        in_specs=[
            pl.BlockSpec((rows, d), lambda l, n: (0, 0)),
            pl.BlockSpec((1, d, tn), lambda l, n: (l, 0, n)),
            pl.BlockSpec((1, 1, tn), lambda l, n: (l, 0, n)),
        ],
        out_specs=pl.BlockSpec((1, rows, tn), lambda l, n: (l, 0, n)),
        out_shape=jax.ShapeDtypeStruct((depth, rows, d6), F32),
        compiler_params=_cparams("arbitrary", "arbitrary"),
        name="adaln_modulation",
    )(c_pad, w_mod, b_mod.reshape(depth, 1, d6))
    return out[:, :bsz]


def _in0_kernel(x_ref, sc_ref, sh_ref, gn_ref, w_ref, vgain_ref, ws_ref, bs_ref, qg_ref, kg_ref,
                cos_ref, sina_ref, sinb_ref, bd128_ref, bd64_ref,
                ya_ref, q_ref, k_ref, v_ref, km_ref):
    tm = x_ref.shape[0]
    gw = GMLP_GROUPS * GMLP_CHUNK
    h = _adaln(x_ref[...], gn_ref[...], sc_ref[0], sh_ref[0])
    proj = _dot(h.astype(BF16), w_ref[...])
    ua, va = proj[:, :gw], proj[:, gw:2 * gw]
    qb, kb, vb = proj[:, 2 * gw:3 * gw], proj[:, 3 * gw:4 * gw], proj[:, 4 * gw:5 * gw]

    u = jax.nn.gelu(ua)
    gv = jax.nn.gelu(va)
    ss = _group_sum(gv * gv, bd128_ref[...]) * (1.0 / GMLP_CHUNK)
    vn = (gv * lax.rsqrt(ss + EPS) * vgain_ref[...]).astype(BF16)
    r = lax.broadcasted_iota(jnp.int32, (GMLP_CHUNK, GMLP_CHUNK), 0)
    c = lax.broadcasted_iota(jnp.int32, (GMLP_CHUNK, GMLP_CHUNK), 1)
    for g in range(GMLP_GROUPS):
        wg = jnp.where(c <= r, ws_ref[g], 0.0).astype(BF16)
        lanes = slice(g * GMLP_CHUNK, (g + 1) * GMLP_CHUNK)
        for ch in range(tm // GMLP_CHUNK):
            rows = slice(ch * GMLP_CHUNK, (ch + 1) * GMLP_CHUNK)
            mixed = _dot(wg, vn[rows, lanes]) + bs_ref[g]
            ya_ref[rows, lanes] = (u[rows, lanes] * mixed).astype(BF16)

    reps = gw // cos_ref.shape[1]
    cos = jnp.concatenate([cos_ref[...]] * reps, axis=1)
    sina = jnp.concatenate([sina_ref[...]] * reps, axis=1)
    sinb = jnp.concatenate([sinb_ref[...]] * reps, axis=1)
    half = ROPE_DIM // 2

    def norm_rope(t, gain):
        ms = _group_sum(t * t, bd64_ref[...]) * (1.0 / HEAD_DIM)
        tn = t * lax.rsqrt(ms + EPS) * gain
        return tn * cos + pltpu.roll(tn, gw - half, 1) * sina + pltpu.roll(tn, half, 1) * sinb

    q = norm_rope(qb, qg_ref[...]) * (HEAD_DIM ** -0.5)
    k = norm_rope(kb, kg_ref[...])
    q_ref[...] = q
    k_ref[...] = k.astype(BF16)
    v_ref[...] = vb.astype(BF16)
    km_ref[0] = jnp.mean(k.reshape(tm // MOBA_BLOCK, MOBA_BLOCK, gw), axis=1)


def _rope_tables(seq):
    half = ROPE_DIM // 2
    inv_freq = ROPE_THETA ** (-jnp.arange(half, dtype=F32) / half)
    ang = jnp.arange(seq, dtype=F32)[:, None] * inv_freq[None, :]
    lane = jnp.arange(2 * HEAD_DIM) % HEAD_DIM
    cos_l = jnp.cos(ang)[:, lane % half]
    sin_l = jnp.sin(ang)[:, lane % half]
    cos = jnp.where(lane < ROPE_DIM, cos_l, 1.0)
    sina = jnp.where(lane < half, -sin_l, 0.0)
    sinb = jnp.where((lane >= half) & (lane < ROPE_DIM), sin_l, 0.0)
    return cos, sina, sinb


def _blockdiag(n, group):
    idx = jnp.arange(n) // group
    return (idx[:, None] == idx[None, :]).astype(BF16)


def _in_proj0(x2d, seq, sc, sh, gain, w_in, v_gain, w_s, b_s, q_gain, k_gain):
    n, d = x2d.shape
    tm = TOKEN_TILE
    tiles_per_seq = seq // tm
    gw = GMLP_GROUPS * GMLP_CHUNK
    n_in = w_in.shape[1]
    cos, sina, sinb = _rope_tables(seq)
    heads = gw // HEAD_DIM
    row_spec = lambda w: pl.BlockSpec((tm, w), lambda i: (i, 0))
    const2 = lambda a: pl.BlockSpec(a.shape, lambda i: (0,) * a.ndim)
    mod_spec = pl.BlockSpec((1, 1, d), lambda i: (i // tiles_per_seq, 0, 0))
    rope_spec = pl.BlockSpec((tm, 2 * HEAD_DIM), lambda i: (i % tiles_per_seq, 0))
    args = [x2d, sc, sh, gain.reshape(1, d), w_in.astype(BF16), v_gain.reshape(1, gw), w_s,
            jnp.broadcast_to(b_s[:, :, None], b_s.shape + (GMLP_CHUNK,)),
            jnp.tile(q_gain, heads).reshape(1, gw), jnp.tile(k_gain, heads).reshape(1, gw),
            cos, sina, sinb, _blockdiag(gw, GMLP_CHUNK), _blockdiag(gw, HEAD_DIM)]
    in_specs = [row_spec(d), mod_spec, mod_spec] + [const2(a) for a in args[3:10]] \
        + [rope_spec] * 3 + [const2(a) for a in args[13:]]
    ya, q, k, v, km = pl.pallas_call(
        _in0_kernel,
        grid=(n // tm,),
        in_specs=in_specs,
        out_specs=[row_spec(gw), row_spec(gw), row_spec(gw), row_spec(gw),
                   pl.BlockSpec((1, tm // MOBA_BLOCK, gw), lambda i: (i, 0, 0))],
        out_shape=[jax.ShapeDtypeStruct((n, gw), BF16), jax.ShapeDtypeStruct((n, gw), F32),
                   jax.ShapeDtypeStruct((n, gw), BF16), jax.ShapeDtypeStruct((n, gw), BF16),
                   jax.ShapeDtypeStruct((n // tm, tm // MOBA_BLOCK, gw), F32)],
        compiler_params=_cparams("arbitrary"),
        name="layer0_in_proj",
    )(*args)
    assert n_in == 5 * gw
    return ya, q, k, v, km.reshape(n // MOBA_BLOCK, gw)


def _moba_kernel(q_ref, k_ref, v_ref, km_ref, o_ref, sel_sc, m_sc, l_sc, acc_sc):
    i = pl.program_id(2)
    t = ATTN_TILE
    pw = 2 * HEAD_DIM
    heads = q_ref.shape[1] // HEAD_DIM
    nb = km_ref.shape[1]
    row = lax.broadcasted_iota(jnp.int32, (t, t), 0)
    col = lax.broadcasted_iota(jnp.int32, (t, t), 1)
    low = lax.broadcasted_iota(jnp.int32, (t, pw), 1) < HEAD_DIM
    blk = lax.broadcasted_iota(jnp.int32, (t, nb), 1)
    past = blk < i

    qm = []
    for h in range(heads):
        lanes = slice((h // 2) * pw, (h // 2 + 1) * pw)
        qh = jnp.where(low if h % 2 == 0 else jnp.logical_not(low), q_ref[:, lanes], 0.0)
        qm.append(qh.astype(BF16))
        gate = lax.dot_general(qh, km_ref[0, :, lanes], NT_DIMS, precision=HIGHEST, preferred_element_type=F32)
        g = jnp.where(past, gate, -jnp.inf)
        kth = None
        for _ in range(MOBA_TOPK):
            kth = jnp.max(g, axis=1, keepdims=True)
            g = jnp.where(g >= kth, -jnp.inf, g)
        sel_sc[h] = jnp.where(past & (gate >= kth), 1.0, 0.0)
    m_sc[...] = jnp.full_like(m_sc, NEG_BIG)
    l_sc[...] = jnp.zeros_like(l_sc)
    acc_sc[...] = jnp.zeros_like(acc_sc)

    def update(jb, diag):
        start = pl.multiple_of(jb * t, t)
        for h in range(heads):
            lanes = slice((h // 2) * pw, (h // 2 + 1) * pw)
            kb = k_ref[0, pl.ds(start, t), lanes]
            vb = v_ref[0, pl.ds(start, t), lanes]
            s = lax.dot_general(qm[h], kb, NT_DIMS, preferred_element_type=F32)
            if diag:
                keep = col <= row
            else:
                keep = jnp.sum(jnp.where(blk == jb, sel_sc[h], 0.0), axis=1, keepdims=True) > 0.5
            s = jnp.where(keep, s, NEG_BIG)
            m_old = m_sc[h]
            m_new = jnp.maximum(m_old, jnp.max(s, axis=1, keepdims=True))
            alpha = jnp.exp(m_old - m_new)
            p = jnp.exp(s - m_new)
            l_sc[h] = alpha * l_sc[h] + jnp.sum(p, axis=1, keepdims=True)
            acc_sc[h] = alpha * acc_sc[h] + _dot(p.astype(BF16), vb)
            m_sc[h] = m_new

    def past_block(jb, carry):
        update(jb, False)
        return carry

    lax.fori_loop(0, i, past_block, 0)
    update(i, True)
    for p in range(heads // 2):
        out = jnp.where(low, acc_sc[2 * p] / l_sc[2 * p], acc_sc[2 * p + 1] / l_sc[2 * p + 1])
        o_ref[:, p * pw:(p + 1) * pw] = out.astype(o_ref.dtype)


def _moba(q, k, v, kmean, bsz, seq):
    n, width = q.shape
    t = ATTN_TILE
    nb = seq // MOBA_BLOCK
    assert t == MOBA_BLOCK
    gw = HEAD_GROUP * HEAD_DIM
    nq = seq // t
    tile = pl.BlockSpec((t, gw), lambda b, g, i: (b * nq + i, g))
    full = pl.BlockSpec((1, seq, gw), lambda b, g, i: (b, 0, g))
    return pl.pallas_call(
        _moba_kernel,
        grid=(bsz, width // gw, nq),
        in_specs=[tile, full, full, pl.BlockSpec((1, nb, gw), lambda b, g, i: (b, 0, g))],
        out_specs=tile,
        out_shape=jax.ShapeDtypeStruct((n, width), BF16),
        scratch_shapes=[pltpu.VMEM((HEAD_GROUP, t, nb), F32), pltpu.VMEM((HEAD_GROUP, t, 1), F32),
                        pltpu.VMEM((HEAD_GROUP, t, 1), F32), pltpu.VMEM((HEAD_GROUP, t, 2 * HEAD_DIM), F32)],
        compiler_params=_cparams("arbitrary", "arbitrary", "arbitrary"),
        name="moba_attention",
    )(q, k.reshape(bsz, seq, width), v.reshape(bsz, seq, width), kmean.reshape(bsz, nb, width))


def _sb_kernel(q_ref, k_ref, v_ref, kabs_ref, tri_ref, o_ref, run_sc, acc_sc):
    i = pl.program_id(2)
    t = ATTN_TILE
    pw = 2 * HEAD_DIM
    heads = q_ref.shape[1] // HEAD_DIM
    tri = tri_ref[...]
    row = lax.broadcasted_iota(jnp.int32, (t, t), 0)
    col = lax.broadcasted_iota(jnp.int32, (t, t), 1)
    strictly_past = col < row
    low = lax.broadcasted_iota(jnp.int32, (t, pw), 1) < HEAD_DIM
    kabs = kabs_ref[0]

    qm, zmax = [], []
    for h in range(heads):
        qp = q_ref[:, (h // 2) * pw:(h // 2 + 1) * pw].astype(F32)
        qh = jnp.where(low if h % 2 == 0 else jnp.logical_not(low), qp, 0.0)
        qm.append(qh.astype(BF16))
        kmax = jnp.max(kabs[:, h * HEAD_DIM:(h + 1) * HEAD_DIM], axis=1, keepdims=True)
        zmax.append(jnp.sum(jnp.abs(qh), axis=1, keepdims=True) * kmax)
    run_sc[...] = jnp.zeros_like(run_sc)
    acc_sc[...] = jnp.zeros_like(acc_sc)

    def block(jb, diag):
        start = pl.multiple_of(jb * t, t)
        worst = None
        for h in range(heads):
            lanes = slice((h // 2) * pw, (h // 2 + 1) * pw)
            kb = k_ref[0, pl.ds(start, t), lanes]
            vb = v_ref[0, pl.ds(start, t), lanes]
            z = lax.dot_general(qm[h], kb, NT_DIMS, preferred_element_type=F32)
            lom = -(jnp.maximum(z, 0.0) + jnp.log(1.0 + jnp.exp(-jnp.abs(z))))
            if diag:
                lom = jnp.where(strictly_past, lom, 0.0)
            hi = lom.astype(BF16)
            lo = (lom - hi.astype(F32)).astype(BF16)
            suffix = _dot(hi, tri) + _dot(lo, tri)
            run = run_sc[h]
            a = jnp.exp(z + suffix + run)
            if diag:
                a = jnp.where(strictly_past, a, 0.0)
            acc_sc[h] += _dot(a.astype(BF16), vb)
            run = run + suffix[:, 0:1]
            run_sc[h] = run
            reach = run + zmax[h]
            worst = reach if worst is None else jnp.maximum(worst, reach)
        return jnp.max(worst)

    def more(c):
        n, worst = c
        return jnp.logical_and(n < i, worst > SB_LOG_FLOOR)

    lax.while_loop(more, lambda c: (c[0] + 1, block(i - 1 - c[0], False)), (jnp.int32(0), block(i, True)))
    for p in range(heads // 2):
        o_ref[:, p * pw:(p + 1) * pw] = jnp.where(low, acc_sc[2 * p], acc_sc[2 * p + 1]).astype(o_ref.dtype)


def _stick_breaking(q, k, v, kabs, bsz, seq):
    n, width = q.shape
    t = ATTN_TILE
    gw = HEAD_GROUP * HEAD_DIM
    nq = seq // t
    idx = jnp.arange(t)
    tri = (idx[:, None] >= idx[None, :]).astype(BF16)
    tile = pl.BlockSpec((t, gw), lambda b, g, i: (b * nq + i, g))
    full = pl.BlockSpec((1, seq, gw), lambda b, g, i: (b, 0, g))
    return pl.pallas_call(
        _sb_kernel,
        grid=(bsz, width // gw, nq),
        in_specs=[tile, full, full, pl.BlockSpec((1, 1, gw), lambda b, g, i: (b, 0, g)),
                  pl.BlockSpec((t, t), lambda b, g, i: (0, 0))],
        out_specs=tile,
        out_shape=jax.ShapeDtypeStruct((n, width), BF16),
        scratch_shapes=[pltpu.VMEM((HEAD_GROUP, t, 1), F32), pltpu.VMEM((HEAD_GROUP, t, 2 * HEAD_DIM), F32)],
        compiler_params=_cparams("arbitrary", "arbitrary", "arbitrary"),
        name="stick_breaking_attention",
    )(q, k.reshape(bsz, seq, width), v.reshape(bsz, seq, width), kabs, tri)


def _in1_kernel(x_ref, sc_ref, sh_ref, gn_ref, w_ref, q_ref, k_ref, v_ref, kabs_ref, *, tiles_per_seq):
    i = pl.program_id(0)
    width = q_ref.shape[1]
    h = _adaln(x_ref[...], gn_ref[...], sc_ref[0], sh_ref[0])
    proj = _dot(h.astype(BF16), w_ref[...])
    q_ref[...] = (proj[:, :width] * (HEAD_DIM ** -0.5)).astype(BF16)
    k = proj[:, width:2 * width].astype(BF16)
    k_ref[...] = k
    v_ref[...] = proj[:, 2 * width:].astype(BF16)
    kabs = jnp.max(jnp.abs(k.astype(F32)), axis=0, keepdims=True)

    @pl.when(i % tiles_per_seq == 0)
    def _first():
        kabs_ref[0] = kabs

    @pl.when(i % tiles_per_seq != 0)
    def _rest():
        kabs_ref[0] = jnp.maximum(kabs_ref[0], kabs)


def _in_proj1(x2d, seq, sc, sh, gain, w_in):
    n, d = x2d.shape
    tm = TOKEN_TILE
    tiles_per_seq = seq // tm
    width = w_in.shape[1] // 3
    row_spec = lambda w: pl.BlockSpec((tm, w), lambda i: (i, 0))
    mod_spec = pl.BlockSpec((1, 1, d), lambda i: (i // tiles_per_seq, 0, 0))
    return pl.pallas_call(
        functools.partial(_in1_kernel, tiles_per_seq=tiles_per_seq),
        grid=(n // tm,),
        in_specs=[row_spec(d), mod_spec, mod_spec, pl.BlockSpec((1, d), lambda i: (0, 0)),
                  pl.BlockSpec(w_in.shape, lambda i: (0, 0))],
        out_specs=[row_spec(width)] * 3 + [pl.BlockSpec((1, 1, width), lambda i: (i // tiles_per_seq, 0, 0))],
        out_shape=[jax.ShapeDtypeStruct((n, width), BF16)] * 3
        + [jax.ShapeDtypeStruct((n // seq, 1, width), F32)],
        compiler_params=_cparams("arbitrary"),
        name="layer1_in_proj",
    )(x2d, sc, sh, gain.reshape(1, d), w_in.astype(BF16))


def _out_kernel(*refs, widths):
    n_in = len(widths)
    y_refs = refs[:n_in]
    w_ref, x_ref, g1_ref, gn_ref, sc_ref, sh_ref, x1_ref, h2t_ref = refs[n_in:]
    y = None
    off = 0
    for y_ref, wd in zip(y_refs, widths):
        part = _dot(y_ref[...], w_ref[off:off + wd, :])
        y = part if y is None else y + part
        off += wd
    x1 = x_ref[...] + g1_ref[0] * y
    x1_ref[...] = x1
    h2 = _adaln(x1, gn_ref[...], sc_ref[0], sh_ref[0])
    h2t_ref[...] = h2.T.astype(BF16)


def _out_proj(ys, w_out, x2d, seq, g1, gain, sc, sh):
    n, d = x2d.shape
    tm = TOKEN_TILE
    tiles_per_seq = seq // tm
    widths = tuple(y.shape[1] for y in ys)
    row_spec = lambda w: pl.BlockSpec((tm, w), lambda i: (i, 0))
    mod_spec = pl.BlockSpec((1, 1, d), lambda i: (i // tiles_per_seq, 0, 0))
    return pl.pallas_call(
        functools.partial(_out_kernel, widths=widths),
        grid=(n // tm,),
        in_specs=[row_spec(w) for w in widths]
        + [pl.BlockSpec(w_out.shape, lambda i: (0, 0)), row_spec(d), mod_spec,
           pl.BlockSpec((1, d), lambda i: (0, 0)), mod_spec, mod_spec],
        out_specs=[row_spec(d), pl.BlockSpec((d, tm), lambda i: (0, i))],
        out_shape=[jax.ShapeDtypeStruct((n, d), F32), jax.ShapeDtypeStruct((d, n), BF16)],
        compiler_params=_cparams("arbitrary"),
        name="out_proj_residual",
    )(*ys, w_out.astype(BF16), x2d, g1, gain.reshape(1, d), sc, sh)


def _top_values(s, n):
    vals = []
    for _ in range(n):
        m = jnp.max(s, axis=0, keepdims=True)
        vals.append(m)
        s = jnp.where(s >= m, -jnp.inf, s)
    return vals


def _peer_kernel(h2t_ref, wqt_ref, sk_ref, u_ref, vt_ref, x1_ref, g2_ref, o_ref,
                 rank_sc, p2_sc, cnt_sc, c1_sc, gate_sc, acc_sc):
    j = pl.program_id(1)
    n_steps = pl.num_programs(1)
    nk = PEER_NKEYS
    kk = PEER_TOPK

    @pl.when(j == 0)
    def _route():
        qt = _dot(wqt_ref[...], h2t_ref[...])
        for h in range(PEER_HEADS):
            s1, s2 = [_dot(sk_ref[h, p], qt[(2 * h + p) * nk:(2 * h + p + 1) * nk, :], precision=HIGHEST)
                      for p in range(2)]
            a1, a2 = _top_values(s1, kk), _top_values(s2, kk)
            cand = jnp.concatenate(
                [a1[r1] + a2[r2] for r1 in range(kk) for r2 in range(kk) if (r1 + 1) * (r2 + 1) <= kk], axis=0)
            best = _top_values(cand, kk)
            tau = best[kk - 1]
            z = jnp.sum(jnp.where(cand >= tau, jnp.exp(cand - best[0]), 0.0), axis=0, keepdims=True)
            rank2 = jnp.zeros_like(s2)
            cnt = jnp.zeros_like(s1)
            for r in range(kk):
                rank2 = rank2 + jnp.where(s2 < a2[r], 1.0, 0.0)
                cnt = cnt + jnp.where(s1 + a2[r] >= tau, 1.0, 0.0)
            e1 = jnp.where(s1 >= a1[kk - 1], jnp.exp(s1 - a1[0]), 0.0)
            e2 = jnp.where(s2 >= a2[kk - 1], jnp.exp(s2 - a2[0]), 0.0)
            rank_sc[h] = rank2.astype(BF16)
            p2_sc[h] = e2.astype(BF16)
            cnt_sc[h] = cnt
            c1_sc[h] = e1 / z
        acc_sc[...] = jnp.zeros_like(acc_sc)

    te = u_ref.shape[0]
    sub = PEER_EXPERT_SUBTILE
    for ai in range(te // nk):
        a = j * (te // nk) + ai
        w = None
        for h in range(PEER_HEADS):
            cnt = cnt_sc[h, pl.ds(a, 1), :].astype(BF16)
            c1 = c1_sc[h, pl.ds(a, 1), :].astype(BF16)
            term = jnp.where(rank_sc[h] < cnt, p2_sc[h] * c1, jnp.zeros((), BF16))
            w = term if w is None else w + term
        gate_sc[ai * nk:(ai + 1) * nk, :] = w
    h2t = h2t_ref[...]
    total = None
    for si in range(te // sub):
        at = _dot(u_ref[si * sub:(si + 1) * sub, :], h2t)
        act = jax.nn.gelu(at).astype(BF16)
        part = _dot(vt_ref[:, si * sub:(si + 1) * sub], act * gate_sc[si * sub:(si + 1) * sub, :])
        total = part if total is None else total + part
    acc_sc[...] += total

    @pl.when(j == n_steps - 1)
    def _finish():
        o_ref[...] = x1_ref[...] + g2_ref[0] * acc_sc[...].T


def _peer(h2t, x1, seq, g2, w_q, subkeys, u_tab, v_tab):
    d, n = h2t.shape
    tm = PEER_TOKEN_TILE
    te = PEER_EXPERT_TILE
    tiles_per_seq = seq // tm
    ne = u_tab.shape[0]
    heads = subkeys.shape[0]
    per_key = (heads, PEER_NKEYS, tm)
    scratch = [pltpu.VMEM(per_key, BF16), pltpu.VMEM(per_key, BF16), pltpu.VMEM(per_key, F32),
               pltpu.VMEM(per_key, F32), pltpu.VMEM((te, tm), BF16), pltpu.VMEM((d, tm), F32)]
    return pl.pallas_call(
        _peer_kernel,
        grid=(n // tm, ne // te),
        in_specs=[
            pl.BlockSpec((d, tm), lambda i, j: (0, i)),
            pl.BlockSpec((w_q.shape[1], d), lambda i, j: (0, 0)),
            pl.BlockSpec(subkeys.shape, lambda i, j: (0, 0, 0, 0)),
            pl.BlockSpec((te, d), lambda i, j: (j, 0)),
            pl.BlockSpec((d, te), lambda i, j: (0, j)),
            pl.BlockSpec((tm, d), lambda i, j: (i, 0)),
            pl.BlockSpec((1, 1, d), lambda i, j: (i // tiles_per_seq, 0, 0)),
        ],
        out_specs=pl.BlockSpec((tm, d), lambda i, j: (i, 0)),
        out_shape=jax.ShapeDtypeStruct((n, d), F32),
        scratch_shapes=scratch,
        compiler_params=_cparams("arbitrary", "arbitrary"),
        name="peer_ffn",
    )(h2t, w_q.T.astype(BF16), subkeys, u_tab.astype(BF16), v_tab.T.astype(BF16), x1, g2)


def kernel(x, c, w_mod, b_mod, norm_mix, norm_ffn, even_w_in, gmlp_v_gain, gmlp_w_s, gmlp_b_s,
           moba_q_gain, moba_k_gain, even_w_out, odd_w_in, odd_w_out, peer_w_q, peer_subkeys,
           peer_u, peer_v):
    bsz, seq, d = x.shape
    depth = w_mod.shape[0]
    assert seq % TOKEN_TILE == 0 and seq % ATTN_TILE == 0 and seq % PEER_TOKEN_TILE == 0
    mod = _modulation(c, w_mod, b_mod)
    xs = x.reshape(bsz * seq, d)
    for layer in range(depth):
        i = layer // 2
        sh1, sc1, g1, sh2, sc2, g2 = [mod[layer, :, k * d:(k + 1) * d].reshape(bsz, 1, d) for k in range(6)]
        if layer % 2 == 0:
            ya, q, k, v, km = _in_proj0(xs, seq, sc1, sh1, norm_mix[layer], even_w_in[i],
                                        gmlp_v_gain[i], gmlp_w_s[i], gmlp_b_s[i],
                                        moba_q_gain[i], moba_k_gain[i])
            yb = _moba(q, k, v, km, bsz, seq)
            ys, w_out = [ya, yb], even_w_out[i]
        else:
            q, k, v, kabs = _in_proj1(xs, seq, sc1, sh1, norm_mix[layer], odd_w_in[i])
            y = _stick_breaking(q, k, v, kabs, bsz, seq)
            ys, w_out = [y], odd_w_out[i]
        x1, h2t = _out_proj(ys, w_out, xs, seq, g1, norm_ffn[layer], sc2, sh2)
        xs = _peer(h2t, x1, seq, g2, peer_w_q[layer], peer_subkeys[layer], peer_u[layer], peer_v[layer])
    return xs.reshape(bsz, seq, d)
```

```python
import functools

import jax
import jax.numpy as jnp
from jax import lax
from jax.experimental import pallas as pl
from jax.experimental.pallas import tpu as pltpu

F32 = jnp.float32
BF16 = jnp.bfloat16
HIGHEST = lax.Precision.HIGHEST

EPS = 1e-6
HEAD_DIM = 64
GMLP_GROUPS = 4
GMLP_CHUNK = 128
MOBA_BLOCK = 256
MOBA_TOPK = 3
ROPE_THETA = 500000.0
ROPE_DIM = HEAD_DIM // 4
PEER_HEADS = 8
PEER_NKEYS = 128
PEER_TOPK = 16
NEG_BIG = -1e30

VMEM_LIMIT_BYTES = 56 * 1024 * 1024
TOKEN_TILE = 512
ATTN_TILE = 256
HEAD_GROUP = 8
SB_LOG_FLOOR = -110.0
PEER_TOKEN_TILE = 512
PEER_EXPERT_TILE = 1024
PEER_EXPERT_SUBTILE = 256
NT_DIMS = (((1,), (1,)), ((), ()))


def _cparams(*sem):
    return pltpu.CompilerParams(dimension_semantics=sem, vmem_limit_bytes=VMEM_LIMIT_BYTES)


def _dot(a, b, **kw):
    return jnp.dot(a, b, preferred_element_type=F32, **kw)


def _group_sum(x, blockdiag):
    hi = x.astype(BF16)
    lo = (x - hi.astype(F32)).astype(BF16)
    return _dot(hi, blockdiag) + _dot(lo, blockdiag)


def _adaln(x, gain, sc, sh):
    ms = jnp.mean(x * x, axis=-1, keepdims=True)
    return x * lax.rsqrt(ms + EPS) * gain * (1.0 + sc) + sh


def _mod_kernel(c_ref, w_ref, b_ref, o_ref):
    c = c_ref[...]
    ca = c * jax.nn.sigmoid(c)
    o_ref[0] = _dot(ca, w_ref[0], precision=HIGHEST) + b_ref[0]


def _modulation(c, w_mod, b_mod):
    depth, d, d6 = w_mod.shape
    bsz = c.shape[0]
    rows = 8
    c_pad = jnp.zeros((rows, d), F32).at[:bsz].set(c)
    tn = 1536
    out = pl.pallas_call(
        _mod_kernel,
        grid=(depth, d6 // tn),
        in_specs=[
            pl.BlockSpec((rows, d), lambda l, n: (0, 0)),
            pl.BlockSpec((1, d, tn), lambda l, n: (l, 0, n)),
            pl.BlockSpec((1, 1, tn), lambda l, n: (l, 0, n)),
        ],
        out_specs=pl.BlockSpec((1, rows, tn), lambda l, n: (l, 0, n)),
        out_shape=jax.ShapeDtypeStruct((depth, rows, d6), F32),
        compiler_params=_cparams("arbitrary", "arbitrary"),
        name="adaln_modulation",
    )(c_pad, w_mod, b_mod.reshape(depth, 1, d6))
    return out[:, :bsz]


def _in0_kernel(x_ref, sc_ref, sh_ref, gn_ref, w_ref, wvt_ref, vgain_ref, ws_ref, bs_ref, qg_ref, kg_ref,
                cos_ref, sina_ref, sinb_ref, bd128_ref, bd64_ref,
                ya_ref, qt_ref, k_ref, vt_ref, km_ref):
    tm = x_ref.shape[0]
    gw = GMLP_GROUPS * GMLP_CHUNK
    h = _adaln(x_ref[...], gn_ref[...], sc_ref[0], sh_ref[0]).astype(BF16)
    proj = _dot(h, w_ref[...])
    ua, va = proj[:, :gw], proj[:, gw:2 * gw]
    qb, kb = proj[:, 2 * gw:3 * gw], proj[:, 3 * gw:4 * gw]
    vt = lax.dot_general(wvt_ref[...], h, NT_DIMS, preferred_element_type=F32).astype(BF16)
    for blk in range(tm // MOBA_BLOCK):
        vt_ref[blk] = vt[:, blk * MOBA_BLOCK:(blk + 1) * MOBA_BLOCK]

    u = jax.nn.gelu(ua)
    gv = jax.nn.gelu(va)
    ss = _group_sum(gv * gv, bd128_ref[...]) * (1.0 / GMLP_CHUNK)
    vn = (gv * lax.rsqrt(ss + EPS) * vgain_ref[...]).astype(BF16)
    r = lax.broadcasted_iota(jnp.int32, (GMLP_CHUNK, GMLP_CHUNK), 0)
    c = lax.broadcasted_iota(jnp.int32, (GMLP_CHUNK, GMLP_CHUNK), 1)
    for g in range(GMLP_GROUPS):
        wg = jnp.where(c <= r, ws_ref[g], 0.0).astype(BF16)
        lanes = slice(g * GMLP_CHUNK, (g + 1) * GMLP_CHUNK)
        for ch in range(tm // GMLP_CHUNK):
            rows = slice(ch * GMLP_CHUNK, (ch + 1) * GMLP_CHUNK)
            mixed = _dot(wg, vn[rows, lanes]) + bs_ref[g]
            ya_ref[rows, lanes] = (u[rows, lanes] * mixed).astype(BF16)

    reps = gw // cos_ref.shape[1]
    cos = jnp.concatenate([cos_ref[...]] * reps, axis=1)
    sina = jnp.concatenate([sina_ref[...]] * reps, axis=1)
    sinb = jnp.concatenate([sinb_ref[...]] * reps, axis=1)
    half = ROPE_DIM // 2

    def norm_rope(t, gain):
        ms = _group_sum(t * t, bd64_ref[...]) * (1.0 / HEAD_DIM)
        tn = t * lax.rsqrt(ms + EPS) * gain
        return tn * cos + pltpu.roll(tn, gw - half, 1) * sina + pltpu.roll(tn, half, 1) * sinb

    q = norm_rope(qb, qg_ref[...]) * (HEAD_DIM ** -0.5)
    k = norm_rope(kb, kg_ref[...])
    qt_ref[...] = q.T
    k_ref[...] = k.astype(BF16)
    km_ref[0] = jnp.mean(k.reshape(tm // MOBA_BLOCK, MOBA_BLOCK, gw), axis=1)


def _rope_tables(seq):
    half = ROPE_DIM // 2
    inv_freq = ROPE_THETA ** (-jnp.arange(half, dtype=F32) / half)
    ang = jnp.arange(seq, dtype=F32)[:, None] * inv_freq[None, :]
    lane = jnp.arange(2 * HEAD_DIM) % HEAD_DIM
    cos_l = jnp.cos(ang)[:, lane % half]
    sin_l = jnp.sin(ang)[:, lane % half]
    cos = jnp.where(lane < ROPE_DIM, cos_l, 1.0)
    sina = jnp.where(lane < half, -sin_l, 0.0)
    sinb = jnp.where((lane >= half) & (lane < ROPE_DIM), sin_l, 0.0)
    return cos, sina, sinb


def _blockdiag(n, group):
    idx = jnp.arange(n) // group
    return (idx[:, None] == idx[None, :]).astype(BF16)


def _in_proj0(x2d, seq, sc, sh, gain, w_in, v_gain, w_s, b_s, q_gain, k_gain):
    n, d = x2d.shape
    tm = TOKEN_TILE
    tiles_per_seq = seq // tm
    gw = GMLP_GROUPS * GMLP_CHUNK
    n_in = w_in.shape[1]
    cos, sina, sinb = _rope_tables(seq)
    heads = gw // HEAD_DIM
    row_spec = lambda w: pl.BlockSpec((tm, w), lambda i: (i, 0))
    const2 = lambda a: pl.BlockSpec(a.shape, lambda i: (0,) * a.ndim)
    mod_spec = pl.BlockSpec((1, 1, d), lambda i: (i // tiles_per_seq, 0, 0))
    rope_spec = pl.BlockSpec((tm, 2 * HEAD_DIM), lambda i: (i % tiles_per_seq, 0))
    assert n_in == 5 * gw
    w_bf = w_in.astype(BF16)
    args = [x2d, sc, sh, gain.reshape(1, d), w_bf[:, :4 * gw], w_bf[:, 4 * gw:].T, v_gain.reshape(1, gw), w_s,
            jnp.broadcast_to(b_s[:, :, None], b_s.shape + (GMLP_CHUNK,)),
            jnp.tile(q_gain, heads).reshape(1, gw), jnp.tile(k_gain, heads).reshape(1, gw),
            cos, sina, sinb, _blockdiag(gw, GMLP_CHUNK), _blockdiag(gw, HEAD_DIM)]
    in_specs = [row_spec(d), mod_spec, mod_spec] + [const2(a) for a in args[3:11]] \
        + [rope_spec] * 3 + [const2(a) for a in args[14:]]
    blocks = tm // MOBA_BLOCK
    ya, qt, k, vt, km = pl.pallas_call(
        _in0_kernel,
        grid=(n // tm,),
        in_specs=in_specs,
        out_specs=[row_spec(gw), pl.BlockSpec((gw, tm), lambda i: (0, i)), row_spec(gw),
                   pl.BlockSpec((blocks, gw, MOBA_BLOCK), lambda i: (i, 0, 0)),
                   pl.BlockSpec((1, blocks, gw), lambda i: (i, 0, 0))],
        out_shape=[jax.ShapeDtypeStruct((n, gw), BF16), jax.ShapeDtypeStruct((gw, n), F32),
                   jax.ShapeDtypeStruct((n, gw), BF16), jax.ShapeDtypeStruct((n // MOBA_BLOCK, gw, MOBA_BLOCK), BF16),
                   jax.ShapeDtypeStruct((n // tm, blocks, gw), F32)],
        compiler_params=_cparams("arbitrary"),
        name="layer0_in_proj",
    )(*args)
    return ya, qt, k, vt, km.reshape(n // MOBA_BLOCK, gw)


def _pair_rows(ref_rows, h):
    rows = lax.broadcasted_iota(jnp.int32, ref_rows.shape, 0)
    mine = rows < HEAD_DIM if h % 2 == 0 else rows >= HEAD_DIM
    return jnp.where(mine, ref_rows, 0.0)


def _moba_kernel(qt_ref, k_ref, vt_ref, km_ref, o_ref, *scratch):
    i = pl.program_id(2)
    t = ATTN_TILE
    pw = 2 * HEAD_DIM
    heads = qt_ref.shape[0] // HEAD_DIM
    sel_sc, m_sc, l_sc, acc_sc = (scratch[n * heads:(n + 1) * heads] for n in range(4))
    nb = km_ref.shape[0]
    key = lax.broadcasted_iota(jnp.int32, (t, t), 0)
    qry = lax.broadcasted_iota(jnp.int32, (t, t), 1)
    blk = lax.broadcasted_iota(jnp.int32, (nb, t), 0)
    past = blk < i

    qm = []
    for h in range(heads):
        feats = slice((h // 2) * pw, (h // 2 + 1) * pw)
        qh = _pair_rows(qt_ref[feats, :], h)
        qm.append(qh.astype(BF16))
        gate = _dot(km_ref[:, feats], qh, precision=HIGHEST)
        g = jnp.where(past, gate, -jnp.inf)
        kth = None
        for _ in range(MOBA_TOPK):
            kth = jnp.max(g, axis=0, keepdims=True)
            g = jnp.where(g >= kth, -jnp.inf, g)
        sel_sc[h][...] = jnp.where(past & (gate >= kth), 1.0, 0.0)
        m_sc[h][...] = jnp.full_like(m_sc[h], NEG_BIG)
        l_sc[h][...] = jnp.zeros_like(l_sc[h])
        acc_sc[h][...] = jnp.zeros_like(acc_sc[h])

    def update(jb, diag):
        start = pl.multiple_of(jb * t, t)
        pair_feats = [slice(p * pw, (p + 1) * pw) for p in range(heads // 2)]
        scores = [_dot(k_ref[0, pl.ds(start, t), pair_feats[h // 2]], qm[h]) for h in range(heads)]
        probs, alphas = [], []
        for h in range(heads):
            keep = (key <= qry) if diag else (sel_sc[h][pl.ds(jb, 1), :] > 0.5)
            s = jnp.where(keep, scores[h], NEG_BIG)
            m_old = m_sc[h][...]
            m_new = jnp.maximum(m_old, jnp.max(s, axis=0, keepdims=True))
            alpha = jnp.exp(m_old - m_new)
            p = jnp.exp(s - m_new)
            l_sc[h][...] = alpha * l_sc[h][...] + jnp.sum(p, axis=0, keepdims=True)
            m_sc[h][...] = m_new
            probs.append(p.astype(BF16))
            alphas.append(alpha)
        for h in range(heads):
            acc_sc[h][...] = alphas[h] * acc_sc[h][...] + _dot(vt_ref[jb, pair_feats[h // 2], :], probs[h])

    def past_block(jb, carry):
        update(jb, False)
        return carry

    lax.fori_loop(0, i, past_block, 0)
    update(i, True)
    top = lax.broadcasted_iota(jnp.int32, (pw, t), 0) < HEAD_DIM
    for p in range(heads // 2):
        out = jnp.where(top, acc_sc[2 * p][...] / l_sc[2 * p][...], acc_sc[2 * p + 1][...] / l_sc[2 * p + 1][...])
        o_ref[:, p * pw:(p + 1) * pw] = out.T.astype(o_ref.dtype)


def _moba(qt, k, vt, kmean, bsz, seq):
    width, n = qt.shape
    t = ATTN_TILE
    nb = seq // MOBA_BLOCK
    assert t == MOBA_BLOCK
    gw = HEAD_GROUP * HEAD_DIM
    nq = seq // t
    return pl.pallas_call(
        _moba_kernel,
        grid=(bsz, width // gw, nq),
        in_specs=[pl.BlockSpec((gw, t), lambda b, g, i: (g, b * nq + i)),
                  pl.BlockSpec((1, seq, gw), lambda b, g, i: (b, 0, g)),
                  pl.BlockSpec((nb, gw, t), lambda b, g, i: (b, g, 0)),
                  pl.BlockSpec((nb, gw), lambda b, g, i: (b, g))],
        out_specs=pl.BlockSpec((t, gw), lambda b, g, i: (b * nq + i, g)),
        out_shape=jax.ShapeDtypeStruct((n, width), BF16),
        scratch_shapes=[pltpu.VMEM(shape, F32) for shape in ((nb, t), (1, t), (1, t), (2 * HEAD_DIM, t))
                        for _ in range(HEAD_GROUP)],
        compiler_params=_cparams("arbitrary", "arbitrary", "arbitrary"),
        name="moba_attention",
    )(qt, k.reshape(bsz, seq, width), vt, kmean)


def _sb_kernel(qt_ref, k_ref, vt_ref, kabs_ref, tri_ref, o_ref, *scratch):
    i = pl.program_id(2)
    t = ATTN_TILE
    pw = 2 * HEAD_DIM
    heads = qt_ref.shape[0] // HEAD_DIM
    run_sc, acc_sc = scratch[:heads], scratch[heads:]
    tri = tri_ref[...]
    key = lax.broadcasted_iota(jnp.int32, (t, t), 0)
    qry = lax.broadcasted_iota(jnp.int32, (t, t), 1)
    strictly_past = key < qry
    kabs = kabs_ref[0]

    qm, zmax = [], []
    for h in range(heads):
        feats = slice((h // 2) * pw, (h // 2 + 1) * pw)
        qh = _pair_rows(qt_ref[feats, :].astype(F32), h)
        qm.append(qh.astype(BF16))
        kmax = jnp.max(kabs[:, h * HEAD_DIM:(h + 1) * HEAD_DIM], axis=1, keepdims=True)
        zmax.append(jnp.sum(jnp.abs(qh), axis=0, keepdims=True) * kmax)
        run_sc[h][...] = jnp.zeros_like(run_sc[h])
        acc_sc[h][...] = jnp.zeros_like(acc_sc[h])

    def block(jb, diag):
        start = pl.multiple_of(jb * t, t)
        pair_feats = [slice(p * pw, (p + 1) * pw) for p in range(heads // 2)]
        zs = [_dot(k_ref[0, pl.ds(start, t), pair_feats[h // 2]], qm[h]) for h in range(heads)]
        his, los = [], []
        for h in range(heads):
            lom = -(jnp.maximum(zs[h], 0.0) + jnp.log(1.0 + jnp.exp(-jnp.abs(zs[h]))))
            if diag:
                lom = jnp.where(strictly_past, lom, 0.0)
            hi = lom.astype(BF16)
            his.append(hi)
            los.append((lom - hi.astype(F32)).astype(BF16))
        suffixes = [_dot(tri, his[h]) + _dot(tri, los[h]) for h in range(heads)]
        weights, worst = [], None
        for h in range(heads):
            run = run_sc[h][...]
            a = jnp.exp(zs[h] + suffixes[h] + run)
            if diag:
                a = jnp.where(strictly_past, a, 0.0)
            weights.append(a.astype(BF16))
            run = run + suffixes[h][0:1, :]
            run_sc[h][...] = run
            reach = run + zmax[h]
            worst = reach if worst is None else jnp.maximum(worst, reach)
        for h in range(heads):
            acc_sc[h][...] += _dot(vt_ref[jb, pair_feats[h // 2], :], weights[h])
        return jnp.max(worst)

    def more(c):
        n, worst = c
        return jnp.logical_and(n < i, worst > SB_LOG_FLOOR)

    lax.while_loop(more, lambda c: (c[0] + 1, block(i - 1 - c[0], False)), (jnp.int32(0), block(i, True)))
    top = lax.broadcasted_iota(jnp.int32, (pw, t), 0) < HEAD_DIM
    for p in range(heads // 2):
        out = jnp.where(top, acc_sc[2 * p][...], acc_sc[2 * p + 1][...])
        o_ref[:, p * pw:(p + 1) * pw] = out.T.astype(o_ref.dtype)


def _stick_breaking(qt, k, vt, kabs, bsz, seq):
    width, n = qt.shape
    t = ATTN_TILE
    gw = HEAD_GROUP * HEAD_DIM
    nq = seq // t
    idx = jnp.arange(t)
    tri = (idx[None, :] >= idx[:, None]).astype(BF16)
    return pl.pallas_call(
        _sb_kernel,
        grid=(bsz, width // gw, nq),
        in_specs=[pl.BlockSpec((gw, t), lambda b, g, i: (g, b * nq + i)),
                  pl.BlockSpec((1, seq, gw), lambda b, g, i: (b, 0, g)),
                  pl.BlockSpec((nq, gw, t), lambda b, g, i: (b, g, 0)),
                  pl.BlockSpec((1, 1, gw), lambda b, g, i: (b, 0, g)),
                  pl.BlockSpec((t, t), lambda b, g, i: (0, 0))],
        out_specs=pl.BlockSpec((t, gw), lambda b, g, i: (b * nq + i, g)),
        out_shape=jax.ShapeDtypeStruct((n, width), BF16),
        scratch_shapes=[pltpu.VMEM(shape, F32) for shape in ((1, t), (2 * HEAD_DIM, t)) for _ in range(HEAD_GROUP)],
        compiler_params=_cparams("arbitrary", "arbitrary", "arbitrary"),
        name="stick_breaking_attention",
    )(qt, k.reshape(bsz, seq, width), vt, kabs, tri)


def _in1_kernel(x_ref, sc_ref, sh_ref, gn_ref, wk_ref, wqt_ref, wvt_ref, qt_ref, k_ref, vt_ref, kabs_ref,
                *, tiles_per_seq):
    i = pl.program_id(0)
    tm = x_ref.shape[0]
    h = _adaln(x_ref[...], gn_ref[...], sc_ref[0], sh_ref[0]).astype(BF16)
    k = _dot(h, wk_ref[...]).astype(BF16)
    k_ref[...] = k
    qt = lax.dot_general(wqt_ref[...], h, NT_DIMS, preferred_element_type=F32)
    qt_ref[...] = (qt * (HEAD_DIM ** -0.5)).astype(BF16)
    vt = lax.dot_general(wvt_ref[...], h, NT_DIMS, preferred_element_type=F32).astype(BF16)
    for blk in range(tm // ATTN_TILE):
        vt_ref[blk] = vt[:, blk * ATTN_TILE:(blk + 1) * ATTN_TILE]
    kabs = jnp.max(jnp.abs(k.astype(F32)), axis=0, keepdims=True)

    @pl.when(i % tiles_per_seq == 0)
    def _first():
        kabs_ref[0] = kabs

    @pl.when(i % tiles_per_seq != 0)
    def _rest():
        kabs_ref[0] = jnp.maximum(kabs_ref[0], kabs)


def _in_proj1(x2d, seq, sc, sh, gain, w_in):
    n, d = x2d.shape
    tm = TOKEN_TILE
    tiles_per_seq = seq // tm
    width = w_in.shape[1] // 3
    blocks = tm // ATTN_TILE
    w_bf = w_in.astype(BF16)
    row_spec = lambda w: pl.BlockSpec((tm, w), lambda i: (i, 0))
    mod_spec = pl.BlockSpec((1, 1, d), lambda i: (i // tiles_per_seq, 0, 0))
    w_spec = pl.BlockSpec((width, d), lambda i: (0, 0))
    return pl.pallas_call(
        functools.partial(_in1_kernel, tiles_per_seq=tiles_per_seq),
        grid=(n // tm,),
        in_specs=[row_spec(d), mod_spec, mod_spec, pl.BlockSpec((1, d), lambda i: (0, 0)),
                  pl.BlockSpec((d, width), lambda i: (0, 0)), w_spec, w_spec],
        out_specs=[pl.BlockSpec((width, tm), lambda i: (0, i)), row_spec(width),
                   pl.BlockSpec((blocks, width, ATTN_TILE), lambda i: (i, 0, 0)),
                   pl.BlockSpec((1, 1, width), lambda i: (i // tiles_per_seq, 0, 0))],
        out_shape=[jax.ShapeDtypeStruct((width, n), BF16), jax.ShapeDtypeStruct((n, width), BF16),
                   jax.ShapeDtypeStruct((n // ATTN_TILE, width, ATTN_TILE), BF16),
                   jax.ShapeDtypeStruct((n // seq, 1, width), F32)],
        compiler_params=_cparams("arbitrary"),
        name="layer1_in_proj",
    )(x2d, sc, sh, gain.reshape(1, d), w_bf[:, width:2 * width], w_bf[:, :width].T, w_bf[:, 2 * width:].T)


def _out_kernel(*refs, widths):
    n_in = len(widths)
    y_refs = refs[:n_in]
    w_ref, x_ref, g1_ref, gn_ref, sc_ref, sh_ref, x1_ref, h2t_ref = refs[n_in:]
    y = None
    off = 0
    for y_ref, wd in zip(y_refs, widths):
        part = _dot(y_ref[...], w_ref[off:off + wd, :])
        y = part if y is None else y + part
        off += wd
    x1 = x_ref[...] + g1_ref[0] * y
    x1_ref[...] = x1
    h2 = _adaln(x1, gn_ref[...], sc_ref[0], sh_ref[0])
    h2t_ref[...] = h2.T.astype(BF16)


def _out_proj(ys, w_out, x2d, seq, g1, gain, sc, sh):
    n, d = x2d.shape
    tm = TOKEN_TILE
    tiles_per_seq = seq // tm
    widths = tuple(y.shape[1] for y in ys)
    row_spec = lambda w: pl.BlockSpec((tm, w), lambda i: (i, 0))
    mod_spec = pl.BlockSpec((1, 1, d), lambda i: (i // tiles_per_seq, 0, 0))
    return pl.pallas_call(
        functools.partial(_out_kernel, widths=widths),
        grid=(n // tm,),
        in_specs=[row_spec(w) for w in widths]
        + [pl.BlockSpec(w_out.shape, lambda i: (0, 0)), row_spec(d), mod_spec,
           pl.BlockSpec((1, d), lambda i: (0, 0)), mod_spec, mod_spec],
        out_specs=[row_spec(d), pl.BlockSpec((d, tm), lambda i: (0, i))],
        out_shape=[jax.ShapeDtypeStruct((n, d), F32), jax.ShapeDtypeStruct((d, n), BF16)],
        compiler_params=_cparams("arbitrary"),
        name="out_proj_residual",
    )(*ys, w_out.astype(BF16), x2d, g1, gain.reshape(1, d), sc, sh)


def _top_values(arrays, n, with_rank=False):
    arrays = list(arrays)
    vals = [[] for _ in arrays]
    ranks = [jnp.full(s.shape, float(n), F32) if with_rank else None for s in arrays]
    for r in range(n):
        for idx, s in enumerate(arrays):
            m = jnp.max(s, axis=0, keepdims=True)
            vals[idx].append(m)
            hit = s >= m
            if with_rank:
                ranks[idx] = jnp.where(hit, float(r), ranks[idx])
            arrays[idx] = jnp.where(hit, -jnp.inf, s)
    return (vals, ranks) if with_rank else vals


def _gelu(x):
    k1 = -2.0 * 0.7978845608028654
    return x / (1.0 + jnp.exp(x * (k1 + (k1 * 0.044715) * (x * x))))


def _peer_kernel(h2t_ref, wqt_ref, sk_ref, u_ref, vt_ref, x1_ref, g2_ref, o_ref,
                 rank_sc, p2_sc, cnt_sc, c1_sc, acc_sc):
    j = pl.program_id(1)
    n_steps = pl.num_programs(1)
    nk = PEER_NKEYS
    kk = PEER_TOPK

    @pl.when(j == 0)
    def _route():
        qt = _dot(wqt_ref[...], h2t_ref[...])
        scores = [_dot(sk_ref[h, p], qt[(2 * h + p) * nk:(2 * h + p + 1) * nk, :], precision=HIGHEST)
                  for h in range(PEER_HEADS) for p in range(2)]
        tops, ranks = _top_values(scores, kk, True)
        cands = [jnp.concatenate([tops[2 * h][r1] + tops[2 * h + 1][r2] for r1 in range(kk) for r2 in range(kk)
                                  if (r1 + 1) * (r2 + 1) <= kk], axis=0) for h in range(PEER_HEADS)]
        bests = _top_values(cands, kk)
        for h in range(PEER_HEADS):
            s1, s2 = scores[2 * h], scores[2 * h + 1]
            a1, a2 = tops[2 * h], tops[2 * h + 1]
            rank1, rank2 = ranks[2 * h], ranks[2 * h + 1]
            tau = bests[h][kk - 1]
            z = jnp.sum(jnp.where(cands[h] >= tau, jnp.exp(cands[h] - bests[h][0]), 0.0), axis=0, keepdims=True)
            top1 = jnp.concatenate(a1, axis=0)
            per_rank = jnp.zeros_like(top1)
            for r2 in range(kk):
                per_rank = per_rank + jnp.where(top1 + a2[r2] >= tau, 1.0, 0.0)
            cnt = jnp.zeros_like(s1)
            for r1 in range(kk):
                cnt = jnp.where(rank1 == float(r1), per_rank[r1:r1 + 1, :], cnt)
            e1 = jnp.where(rank1 < kk, jnp.exp(s1 - a1[0]), 0.0)
            e2 = jnp.where(rank2 < kk, jnp.exp(s2 - a2[0]), 0.0)
            rank_sc[h] = rank2.astype(BF16)
            p2_sc[h] = e2.astype(BF16)
            cnt_sc[h] = cnt
            c1_sc[h] = e1 / z
        acc_sc[...] = jnp.zeros_like(acc_sc)

    te = u_ref.shape[0]
    sub = PEER_EXPERT_SUBTILE
    gates = []
    for ai in range(te // nk):
        a = j * (te // nk) + ai
        w = None
        for h in range(PEER_HEADS):
            cnt = cnt_sc[h, pl.ds(a, 1), :].astype(BF16)
            c1 = c1_sc[h, pl.ds(a, 1), :].astype(BF16)
            term = jnp.where(rank_sc[h] < cnt, p2_sc[h] * c1, jnp.zeros((), BF16))
            w = term if w is None else w + term
        gates.append(w)
    h2t = h2t_ref[...]
    pre = [_dot(u_ref[si * sub:(si + 1) * sub, :], h2t) for si in range(te // sub)]
    per_sub = sub // nk
    b = jnp.concatenate(
        [_gelu(pre[si]).astype(BF16) * jnp.concatenate(gates[si * per_sub:(si + 1) * per_sub], axis=0)
         for si in range(te // sub)], axis=0)
    acc_sc[...] += _dot(vt_ref[...], b)

    @pl.when(j == n_steps - 1)
    def _finish():
        o_ref[...] = x1_ref[...] + g2_ref[0] * acc_sc[...].T


def _peer(h2t, x1, seq, g2, w_q, subkeys, u_tab, v_tab):
    d, n = h2t.shape
    tm = PEER_TOKEN_TILE
    te = PEER_EXPERT_TILE
    tiles_per_seq = seq // tm
    ne = u_tab.shape[0]
    heads = subkeys.shape[0]
    per_key = (heads, PEER_NKEYS, tm)
    scratch = [pltpu.VMEM(per_key, BF16), pltpu.VMEM(per_key, BF16), pltpu.VMEM(per_key, F32),
               pltpu.VMEM(per_key, F32), pltpu.VMEM((d, tm), F32)]
    return pl.pallas_call(
        _peer_kernel,
        grid=(n // tm, ne // te),
        in_specs=[
            pl.BlockSpec((d, tm), lambda i, j: (0, i)),
            pl.BlockSpec((w_q.shape[1], d), lambda i, j: (0, 0)),
            pl.BlockSpec(subkeys.shape, lambda i, j: (0, 0, 0, 0)),
            pl.BlockSpec((te, d), lambda i, j: (j, 0)),
            pl.BlockSpec((d, te), lambda i, j: (0, j)),
            pl.BlockSpec((tm, d), lambda i, j: (i, 0)),
            pl.BlockSpec((1, 1, d), lambda i, j: (i // tiles_per_seq, 0, 0)),
        ],
        out_specs=pl.BlockSpec((tm, d), lambda i, j: (i, 0)),
        out_shape=jax.ShapeDtypeStruct((n, d), F32),
        scratch_shapes=scratch,
        compiler_params=_cparams("arbitrary", "arbitrary"),
        name="peer_ffn",
    )(h2t, w_q.T.astype(BF16), subkeys, u_tab.astype(BF16), v_tab.T.astype(BF16), x1, g2)


def kernel(x, c, w_mod, b_mod, norm_mix, norm_ffn, even_w_in, gmlp_v_gain, gmlp_w_s, gmlp_b_s,
           moba_q_gain, moba_k_gain, even_w_out, odd_w_in, odd_w_out, peer_w_q, peer_subkeys,
           peer_u, peer_v):
    bsz, seq, d = x.shape
    depth = w_mod.shape[0]
    assert seq % TOKEN_TILE == 0 and seq % ATTN_TILE == 0 and seq % PEER_TOKEN_TILE == 0
    mod = _modulation(c, w_mod, b_mod)
    xs = x.reshape(bsz * seq, d)
    for layer in range(depth):
        i = layer // 2
        sh1, sc1, g1, sh2, sc2, g2 = [mod[layer, :, k * d:(k + 1) * d].reshape(bsz, 1, d) for k in range(6)]
        if layer % 2 == 0:
            ya, qt, k, vt, km = _in_proj0(xs, seq, sc1, sh1, norm_mix[layer], even_w_in[i],
                                          gmlp_v_gain[i], gmlp_w_s[i], gmlp_b_s[i],
                                          moba_q_gain[i], moba_k_gain[i])
            yb = _moba(qt, k, vt, km, bsz, seq)
            ys, w_out = [ya, yb], even_w_out[i]
        else:
            qt, k, vt, kabs = _in_proj1(xs, seq, sc1, sh1, norm_mix[layer], odd_w_in[i])
            y = _stick_breaking(qt, k, vt, kabs, bsz, seq)
            ys, w_out = [y], odd_w_out[i]
        x1, h2t = _out_proj(ys, w_out, xs, seq, g1, norm_ffn[layer], sc2, sh2)
        xs = _peer(h2t, x1, seq, g2, peer_w_q[layer], peer_subkeys[layer], peer_u[layer], peer_v[layer])
    return xs.reshape(bsz, seq, d)
```

```python
import functools

import jax
import jax.numpy as jnp
from jax import lax
from jax.experimental import pallas as pl
from jax.experimental.pallas import tpu as pltpu

F32 = jnp.float32
BF16 = jnp.bfloat16
HIGHEST = lax.Precision.HIGHEST

EPS = 1e-6
HEAD_DIM = 64
GMLP_GROUPS = 4
GMLP_CHUNK = 128
MOBA_BLOCK = 256
MOBA_TOPK = 3
ROPE_THETA = 500000.0
ROPE_DIM = HEAD_DIM // 4
PEER_HEADS = 8
PEER_NKEYS = 128
PEER_TOPK = 16
NEG_BIG = -1e30

VMEM_LIMIT_BYTES = 56 * 1024 * 1024
TOKEN_TILE = 512
ATTN_TILE = 256
HEAD_GROUP = 8
SB_LOG_FLOOR = -110.0
PEER_TOKEN_TILE = 512
PEER_EXPERT_TILE = 2048
PEER_EXPERT_SUBTILE = 256
NT_DIMS = (((1,), (1,)), ((), ()))


def _cparams(*sem):
    return pltpu.CompilerParams(dimension_semantics=sem, vmem_limit_bytes=VMEM_LIMIT_BYTES)


def _dot(a, b, **kw):
    return jnp.dot(a, b, preferred_element_type=F32, **kw)


def _group_sum(x, blockdiag):
    hi = x.astype(BF16)
    lo = (x - hi.astype(F32)).astype(BF16)
    return _dot(hi, blockdiag) + _dot(lo, blockdiag)


def _adaln(x, gain, sc, sh):
    ms = jnp.mean(x * x, axis=-1, keepdims=True)
    return x * lax.rsqrt(ms + EPS) * gain * (1.0 + sc) + sh


def _mod_kernel(c_ref, w_ref, b_ref, o_ref):
    c = c_ref[...]
    ca = c * jax.nn.sigmoid(c)
    o_ref[0] = _dot(ca, w_ref[0], precision=HIGHEST) + b_ref[0]


def _modulation(c, w_mod, b_mod):
    depth, d, d6 = w_mod.shape
    bsz = c.shape[0]
    rows = 8
    c_pad = jnp.zeros((rows, d), F32).at[:bsz].set(c)
    tn = 1536
    out = pl.pallas_call(
        _mod_kernel,
        grid=(depth, d6 // tn),
        in_specs=[
            pl.BlockSpec((rows, d), lambda l, n: (0, 0)),
            pl.BlockSpec((1, d, tn), lambda l, n: (l, 0, n)),
            pl.BlockSpec((1, 1, tn), lambda l, n: (l, 0, n)),
        ],
        out_specs=pl.BlockSpec((1, rows, tn), lambda l, n: (l, 0, n)),
        out_shape=jax.ShapeDtypeStruct((depth, rows, d6), F32),
        compiler_params=_cparams("arbitrary", "arbitrary"),
        name="adaln_modulation",
    )(c_pad, w_mod, b_mod.reshape(depth, 1, d6))
    return out[:, :bsz]


def _in0_kernel(x_ref, sc_ref, sh_ref, gn_ref, w_ref, wvt_ref, vgain_ref, ws_ref, bs_ref, qg_ref, kg_ref,
                cos_ref, sina_ref, sinb_ref, bd128_ref, bd64_ref,
                ya_ref, qt_ref, k_ref, vt_ref, km_ref):
    tm = x_ref.shape[0]
    gw = GMLP_GROUPS * GMLP_CHUNK
    h = _adaln(x_ref[...], gn_ref[...], sc_ref[0], sh_ref[0]).astype(BF16)
    proj = _dot(h, w_ref[...])
    ua, va = proj[:, :gw], proj[:, gw:2 * gw]
    qb, kb = proj[:, 2 * gw:3 * gw], proj[:, 3 * gw:4 * gw]
    vt = lax.dot_general(wvt_ref[...], h, NT_DIMS, preferred_element_type=F32).astype(BF16)
    for blk in range(tm // MOBA_BLOCK):
        vt_ref[blk] = vt[:, blk * MOBA_BLOCK:(blk + 1) * MOBA_BLOCK]

    u = jax.nn.gelu(ua)
    gv = jax.nn.gelu(va)
    ss = _group_sum(gv * gv, bd128_ref[...]) * (1.0 / GMLP_CHUNK)
    vn = (gv * lax.rsqrt(ss + EPS) * vgain_ref[...]).astype(BF16)
    r = lax.broadcasted_iota(jnp.int32, (GMLP_CHUNK, GMLP_CHUNK), 0)
    c = lax.broadcasted_iota(jnp.int32, (GMLP_CHUNK, GMLP_CHUNK), 1)
    for g in range(GMLP_GROUPS):
        wg = jnp.where(c <= r, ws_ref[g], 0.0).astype(BF16)
        lanes = slice(g * GMLP_CHUNK, (g + 1) * GMLP_CHUNK)
        for ch in range(tm // GMLP_CHUNK):
            rows = slice(ch * GMLP_CHUNK, (ch + 1) * GMLP_CHUNK)
            mixed = _dot(wg, vn[rows, lanes]) + bs_ref[g]
            ya_ref[rows, lanes] = (u[rows, lanes] * mixed).astype(BF16)

    reps = gw // cos_ref.shape[1]
    cos = jnp.concatenate([cos_ref[...]] * reps, axis=1)
    sina = jnp.concatenate([sina_ref[...]] * reps, axis=1)
    sinb = jnp.concatenate([sinb_ref[...]] * reps, axis=1)
    half = ROPE_DIM // 2

    def norm_rope(t, gain):
        ms = _group_sum(t * t, bd64_ref[...]) * (1.0 / HEAD_DIM)
        tn = t * lax.rsqrt(ms + EPS) * gain
        return tn * cos + pltpu.roll(tn, gw - half, 1) * sina + pltpu.roll(tn, half, 1) * sinb

    q = norm_rope(qb, qg_ref[...]) * (HEAD_DIM ** -0.5)
    k = norm_rope(kb, kg_ref[...])
    qt_ref[...] = q.T
    k_ref[...] = k.astype(BF16)
    km_ref[0] = jnp.mean(k.reshape(tm // MOBA_BLOCK, MOBA_BLOCK, gw), axis=1)


def _rope_tables(seq):
    half = ROPE_DIM // 2
    inv_freq = ROPE_THETA ** (-jnp.arange(half, dtype=F32) / half)
    ang = jnp.arange(seq, dtype=F32)[:, None] * inv_freq[None, :]
    lane = jnp.arange(2 * HEAD_DIM) % HEAD_DIM
    cos_l = jnp.cos(ang)[:, lane % half]
    sin_l = jnp.sin(ang)[:, lane % half]
    cos = jnp.where(lane < ROPE_DIM, cos_l, 1.0)
    sina = jnp.where(lane < half, -sin_l, 0.0)
    sinb = jnp.where((lane >= half) & (lane < ROPE_DIM), sin_l, 0.0)
    return cos, sina, sinb


def _blockdiag(n, group):
    idx = jnp.arange(n) // group
    return (idx[:, None] == idx[None, :]).astype(BF16)


def _in_proj0(x2d, seq, sc, sh, gain, w_in, v_gain, w_s, b_s, q_gain, k_gain):
    n, d = x2d.shape
    tm = TOKEN_TILE
    tiles_per_seq = seq // tm
    gw = GMLP_GROUPS * GMLP_CHUNK
    n_in = w_in.shape[1]
    cos, sina, sinb = _rope_tables(seq)
    heads = gw // HEAD_DIM
    row_spec = lambda w: pl.BlockSpec((tm, w), lambda i: (i, 0))
    const2 = lambda a: pl.BlockSpec(a.shape, lambda i: (0,) * a.ndim)
    mod_spec = pl.BlockSpec((1, 1, d), lambda i: (i // tiles_per_seq, 0, 0))
    rope_spec = pl.BlockSpec((tm, 2 * HEAD_DIM), lambda i: (i % tiles_per_seq, 0))
    assert n_in == 5 * gw
    w_bf = w_in.astype(BF16)
    args = [x2d, sc, sh, gain.reshape(1, d), w_bf[:, :4 * gw], w_bf[:, 4 * gw:].T, v_gain.reshape(1, gw), w_s,
            jnp.broadcast_to(b_s[:, :, None], b_s.shape + (GMLP_CHUNK,)),
            jnp.tile(q_gain, heads).reshape(1, gw), jnp.tile(k_gain, heads).reshape(1, gw),
            cos, sina, sinb, _blockdiag(gw, GMLP_CHUNK), _blockdiag(gw, HEAD_DIM)]
    in_specs = [row_spec(d), mod_spec, mod_spec] + [const2(a) for a in args[3:11]] \
        + [rope_spec] * 3 + [const2(a) for a in args[14:]]
    blocks = tm // MOBA_BLOCK
    ya, qt, k, vt, km = pl.pallas_call(
        _in0_kernel,
        grid=(n // tm,),
        in_specs=in_specs,
        out_specs=[row_spec(gw), pl.BlockSpec((gw, tm), lambda i: (0, i)), row_spec(gw),
                   pl.BlockSpec((blocks, gw, MOBA_BLOCK), lambda i: (i, 0, 0)),
                   pl.BlockSpec((1, blocks, gw), lambda i: (i, 0, 0))],
        out_shape=[jax.ShapeDtypeStruct((n, gw), BF16), jax.ShapeDtypeStruct((gw, n), F32),
                   jax.ShapeDtypeStruct((n, gw), BF16), jax.ShapeDtypeStruct((n // MOBA_BLOCK, gw, MOBA_BLOCK), BF16),
                   jax.ShapeDtypeStruct((n // tm, blocks, gw), F32)],
        compiler_params=_cparams("arbitrary"),
        name="layer0_in_proj",
    )(*args)
    return ya, qt, k, vt, km.reshape(n // MOBA_BLOCK, gw)


def _pair_rows(ref_rows, h):
    rows = lax.broadcasted_iota(jnp.int32, ref_rows.shape, 0)
    mine = rows < HEAD_DIM if h % 2 == 0 else rows >= HEAD_DIM
    return jnp.where(mine, ref_rows, 0.0)


def _moba_kernel(qt_ref, k_ref, vt_ref, km_ref, o_ref, *scratch):
    i = pl.program_id(2)
    t = ATTN_TILE
    pw = 2 * HEAD_DIM
    heads = qt_ref.shape[0] // HEAD_DIM
    sel_sc, m_sc, l_sc, acc_sc = (scratch[n * heads:(n + 1) * heads] for n in range(4))
    nb = km_ref.shape[0]
    key = lax.broadcasted_iota(jnp.int32, (t, t), 0)
    qry = lax.broadcasted_iota(jnp.int32, (t, t), 1)
    blk = lax.broadcasted_iota(jnp.int32, (nb, t), 0)
    past = blk < i

    qm = []
    for h in range(heads):
        feats = slice((h // 2) * pw, (h // 2 + 1) * pw)
        qh = _pair_rows(qt_ref[feats, :], h)
        qm.append(qh.astype(BF16))
        gate = _dot(km_ref[:, feats], qh, precision=HIGHEST)
        g = jnp.where(past, gate, -jnp.inf)
        kth = None
        for _ in range(MOBA_TOPK):
            kth = jnp.max(g, axis=0, keepdims=True)
            g = jnp.where(g >= kth, -jnp.inf, g)
        sel_sc[h][...] = jnp.where(past & (gate >= kth), 1.0, 0.0)
        m_sc[h][...] = jnp.full_like(m_sc[h], NEG_BIG)
        l_sc[h][...] = jnp.zeros_like(l_sc[h])
        acc_sc[h][...] = jnp.zeros_like(acc_sc[h])

    def update(jb, diag):
        start = pl.multiple_of(jb * t, t)
        pair_feats = [slice(p * pw, (p + 1) * pw) for p in range(heads // 2)]
        scores = [_dot(k_ref[0, pl.ds(start, t), pair_feats[h // 2]], qm[h]) for h in range(heads)]
        probs, alphas = [], []
        for h in range(heads):
            keep = (key <= qry) if diag else (sel_sc[h][pl.ds(jb, 1), :] > 0.5)
            s = jnp.where(keep, scores[h], NEG_BIG)
            m_old = m_sc[h][...]
            m_new = jnp.maximum(m_old, jnp.max(s, axis=0, keepdims=True))
            alpha = jnp.exp(m_old - m_new)
            p = jnp.exp(s - m_new)
            l_sc[h][...] = alpha * l_sc[h][...] + jnp.sum(p, axis=0, keepdims=True)
            m_sc[h][...] = m_new
            probs.append(p.astype(BF16))
            alphas.append(alpha)
        for h in range(heads):
            acc_sc[h][...] = alphas[h] * acc_sc[h][...] + _dot(vt_ref[jb, pair_feats[h // 2], :], probs[h])

    def past_block(jb, carry):
        update(jb, False)
        return carry

    lax.fori_loop(0, i, past_block, 0)
    update(i, True)
    top = lax.broadcasted_iota(jnp.int32, (pw, t), 0) < HEAD_DIM
    for p in range(heads // 2):
        out = jnp.where(top, acc_sc[2 * p][...] / l_sc[2 * p][...], acc_sc[2 * p + 1][...] / l_sc[2 * p + 1][...])
        o_ref[:, p * pw:(p + 1) * pw] = out.T.astype(o_ref.dtype)


def _moba(qt, k, vt, kmean, bsz, seq):
    width, n = qt.shape
    t = ATTN_TILE
    nb = seq // MOBA_BLOCK
    assert t == MOBA_BLOCK
    gw = HEAD_GROUP * HEAD_DIM
    nq = seq // t
    return pl.pallas_call(
        _moba_kernel,
        grid=(bsz, width // gw, nq),
        in_specs=[pl.BlockSpec((gw, t), lambda b, g, i: (g, b * nq + i)),
                  pl.BlockSpec((1, seq, gw), lambda b, g, i: (b, 0, g)),
                  pl.BlockSpec((nb, gw, t), lambda b, g, i: (b, g, 0)),
                  pl.BlockSpec((nb, gw), lambda b, g, i: (b, g))],
        out_specs=pl.BlockSpec((t, gw), lambda b, g, i: (b * nq + i, g)),
        out_shape=jax.ShapeDtypeStruct((n, width), BF16),
        scratch_shapes=[pltpu.VMEM(shape, F32) for shape in ((nb, t), (1, t), (1, t), (2 * HEAD_DIM, t))
                        for _ in range(HEAD_GROUP)],
        compiler_params=_cparams("arbitrary", "arbitrary", "arbitrary"),
        name="moba_attention",
    )(qt, k.reshape(bsz, seq, width), vt, kmean)


def _sb_kernel(qt_ref, k_ref, vt_ref, kabs_ref, tri_ref, o_ref, *scratch):
    i = pl.program_id(2)
    t = ATTN_TILE
    pw = 2 * HEAD_DIM
    heads = qt_ref.shape[0] // HEAD_DIM
    run_sc, acc_sc = scratch[:heads], scratch[heads:]
    tri = tri_ref[...]
    key = lax.broadcasted_iota(jnp.int32, (t, t), 0)
    qry = lax.broadcasted_iota(jnp.int32, (t, t), 1)
    strictly_past = key < qry
    kabs = kabs_ref[0]

    qm, zmax = [], []
    for h in range(heads):
        feats = slice((h // 2) * pw, (h // 2 + 1) * pw)
        qh = _pair_rows(qt_ref[feats, :].astype(F32), h)
        qm.append(qh.astype(BF16))
        kmax = jnp.max(kabs[:, h * HEAD_DIM:(h + 1) * HEAD_DIM], axis=1, keepdims=True)
        zmax.append(jnp.sum(jnp.abs(qh), axis=0, keepdims=True) * kmax)
        run_sc[h][...] = jnp.zeros_like(run_sc[h])
        acc_sc[h][...] = jnp.zeros_like(acc_sc[h])

    def block(jb, diag):
        start = pl.multiple_of(jb * t, t)
        pair_feats = [slice(p * pw, (p + 1) * pw) for p in range(heads // 2)]
        zs = [_dot(k_ref[0, pl.ds(start, t), pair_feats[h // 2]], qm[h]) for h in range(heads)]
        his, los = [], []
        for h in range(heads):
            lom = -(jnp.maximum(zs[h], 0.0) + jnp.log(1.0 + jnp.exp(-jnp.abs(zs[h]))))
            if diag:
                lom = jnp.where(strictly_past, lom, 0.0)
            hi = lom.astype(BF16)
            his.append(hi)
            los.append((lom - hi.astype(F32)).astype(BF16))
        suffixes = [_dot(tri, his[h]) + _dot(tri, los[h]) for h in range(heads)]
        weights, worst = [], None
        for h in range(heads):
            run = run_sc[h][...]
            a = jnp.exp(zs[h] + suffixes[h] + run)
            if diag:
                a = jnp.where(strictly_past, a, 0.0)
            weights.append(a.astype(BF16))
            run = run + suffixes[h][0:1, :]
            run_sc[h][...] = run
            reach = run + zmax[h]
            worst = reach if worst is None else jnp.maximum(worst, reach)
        for h in range(heads):
            acc_sc[h][...] += _dot(vt_ref[jb, pair_feats[h // 2], :], weights[h])
        return jnp.max(worst)

    def more(c):
        n, worst = c
        return jnp.logical_and(n < i, worst > SB_LOG_FLOOR)

    lax.while_loop(more, lambda c: (c[0] + 1, block(i - 1 - c[0], False)), (jnp.int32(0), block(i, True)))
    top = lax.broadcasted_iota(jnp.int32, (pw, t), 0) < HEAD_DIM
    for p in range(heads // 2):
        out = jnp.where(top, acc_sc[2 * p][...], acc_sc[2 * p + 1][...])
        o_ref[:, p * pw:(p + 1) * pw] = out.T.astype(o_ref.dtype)


def _stick_breaking(qt, k, vt, kabs, bsz, seq):
    width, n = qt.shape
    t = ATTN_TILE
    gw = HEAD_GROUP * HEAD_DIM
    nq = seq // t
    idx = jnp.arange(t)
    tri = (idx[None, :] >= idx[:, None]).astype(BF16)
    return pl.pallas_call(
        _sb_kernel,
        grid=(bsz, width // gw, nq),
        in_specs=[pl.BlockSpec((gw, t), lambda b, g, i: (g, b * nq + i)),
                  pl.BlockSpec((1, seq, gw), lambda b, g, i: (b, 0, g)),
                  pl.BlockSpec((nq, gw, t), lambda b, g, i: (b, g, 0)),
                  pl.BlockSpec((1, 1, gw), lambda b, g, i: (b, 0, g)),
                  pl.BlockSpec((t, t), lambda b, g, i: (0, 0))],
        out_specs=pl.BlockSpec((t, gw), lambda b, g, i: (b * nq + i, g)),
        out_shape=jax.ShapeDtypeStruct((n, width), BF16),
        scratch_shapes=[pltpu.VMEM(shape, F32) for shape in ((1, t), (2 * HEAD_DIM, t)) for _ in range(HEAD_GROUP)],
        compiler_params=_cparams("arbitrary", "arbitrary", "arbitrary"),
        name="stick_breaking_attention",
    )(qt, k.reshape(bsz, seq, width), vt, kabs, tri)


def _in1_kernel(x_ref, sc_ref, sh_ref, gn_ref, wk_ref, wqt_ref, wvt_ref, qt_ref, k_ref, vt_ref, kabs_ref,
                *, tiles_per_seq):
    i = pl.program_id(0)
    tm = x_ref.shape[0]
    h = _adaln(x_ref[...], gn_ref[...], sc_ref[0], sh_ref[0]).astype(BF16)
    k = _dot(h, wk_ref[...]).astype(BF16)
    k_ref[...] = k
    qt = lax.dot_general(wqt_ref[...], h, NT_DIMS, preferred_element_type=F32)
    qt_ref[...] = (qt * (HEAD_DIM ** -0.5)).astype(BF16)
    vt = lax.dot_general(wvt_ref[...], h, NT_DIMS, preferred_element_type=F32).astype(BF16)
    for blk in range(tm // ATTN_TILE):
        vt_ref[blk] = vt[:, blk * ATTN_TILE:(blk + 1) * ATTN_TILE]
    kabs = jnp.max(jnp.abs(k.astype(F32)), axis=0, keepdims=True)

    @pl.when(i % tiles_per_seq == 0)
    def _first():
        kabs_ref[0] = kabs

    @pl.when(i % tiles_per_seq != 0)
    def _rest():
        kabs_ref[0] = jnp.maximum(kabs_ref[0], kabs)


def _in_proj1(x2d, seq, sc, sh, gain, w_in):
    n, d = x2d.shape
    tm = TOKEN_TILE
    tiles_per_seq = seq // tm
    width = w_in.shape[1] // 3
    blocks = tm // ATTN_TILE
    w_bf = w_in.astype(BF16)
    row_spec = lambda w: pl.BlockSpec((tm, w), lambda i: (i, 0))
    mod_spec = pl.BlockSpec((1, 1, d), lambda i: (i // tiles_per_seq, 0, 0))
    w_spec = pl.BlockSpec((width, d), lambda i: (0, 0))
    return pl.pallas_call(
        functools.partial(_in1_kernel, tiles_per_seq=tiles_per_seq),
        grid=(n // tm,),
        in_specs=[row_spec(d), mod_spec, mod_spec, pl.BlockSpec((1, d), lambda i: (0, 0)),
                  pl.BlockSpec((d, width), lambda i: (0, 0)), w_spec, w_spec],
        out_specs=[pl.BlockSpec((width, tm), lambda i: (0, i)), row_spec(width),
                   pl.BlockSpec((blocks, width, ATTN_TILE), lambda i: (i, 0, 0)),
                   pl.BlockSpec((1, 1, width), lambda i: (i // tiles_per_seq, 0, 0))],
        out_shape=[jax.ShapeDtypeStruct((width, n), BF16), jax.ShapeDtypeStruct((n, width), BF16),
                   jax.ShapeDtypeStruct((n // ATTN_TILE, width, ATTN_TILE), BF16),
                   jax.ShapeDtypeStruct((n // seq, 1, width), F32)],
        compiler_params=_cparams("arbitrary"),
        name="layer1_in_proj",
    )(x2d, sc, sh, gain.reshape(1, d), w_bf[:, width:2 * width], w_bf[:, :width].T, w_bf[:, 2 * width:].T)


def _out_kernel(*refs, widths):
    n_in = len(widths)
    y_refs = refs[:n_in]
    w_ref, x_ref, g1_ref, gn_ref, sc_ref, sh_ref, x1_ref, h2t_ref = refs[n_in:]
    y = None
    off = 0
    for y_ref, wd in zip(y_refs, widths):
        part = _dot(y_ref[...], w_ref[off:off + wd, :])
        y = part if y is None else y + part
        off += wd
    x1 = x_ref[...] + g1_ref[0] * y
    x1_ref[...] = x1
    h2 = _adaln(x1, gn_ref[...], sc_ref[0], sh_ref[0])
    h2t_ref[...] = h2.T.astype(BF16)


def _out_proj(ys, w_out, x2d, seq, g1, gain, sc, sh):
    n, d = x2d.shape
    tm = TOKEN_TILE
    tiles_per_seq = seq // tm
    widths = tuple(y.shape[1] for y in ys)
    row_spec = lambda w: pl.BlockSpec((tm, w), lambda i: (i, 0))
    mod_spec = pl.BlockSpec((1, 1, d), lambda i: (i // tiles_per_seq, 0, 0))
    return pl.pallas_call(
        functools.partial(_out_kernel, widths=widths),
        grid=(n // tm,),
        in_specs=[row_spec(w) for w in widths]
        + [pl.BlockSpec(w_out.shape, lambda i: (0, 0)), row_spec(d), mod_spec,
           pl.BlockSpec((1, d), lambda i: (0, 0)), mod_spec, mod_spec],
        out_specs=[row_spec(d), pl.BlockSpec((d, tm), lambda i: (0, i))],
        out_shape=[jax.ShapeDtypeStruct((n, d), F32), jax.ShapeDtypeStruct((d, n), BF16)],
        compiler_params=_cparams("arbitrary"),
        name="out_proj_residual",
    )(*ys, w_out.astype(BF16), x2d, g1, gain.reshape(1, d), sc, sh)


def _top_values(arrays, n):
    arrays = list(arrays)
    vals = [[] for _ in arrays]
    for _ in range(n):
        for idx, s in enumerate(arrays):
            m = jnp.max(s, axis=0, keepdims=True)
            vals[idx].append(m)
            arrays[idx] = jnp.where(s >= m, -jnp.inf, s)
    return vals


def _oddeven_merge(lo, hi, r):
    step = r * 2
    if step < hi - lo:
        yield from _oddeven_merge(lo, hi, step)
        yield from _oddeven_merge(lo + r, hi, step)
        for i in range(lo + r, hi - r, step):
            yield (i, i + r)
    else:
        yield (lo, lo + r)


def _oddeven_merge_sort(lo, hi):
    if hi - lo >= 1:
        mid = lo + (hi - lo) // 2
        yield from _oddeven_merge_sort(lo, mid)
        yield from _oddeven_merge_sort(mid + 1, hi)
        yield from _oddeven_merge(lo, hi, 1)


F32_SUBLANES = 8


def _top_sorted(s):
    n = s.shape[0] // F32_SUBLANES
    x = [s[F32_SUBLANES * i:F32_SUBLANES * (i + 1), :] for i in range(n)]

    def exchange(i, j):
        x[i], x[j] = jnp.maximum(x[i], x[j]), jnp.minimum(x[i], x[j])

    for i, j in _oddeven_merge_sort(0, n - 1):
        exchange(i, j)
    shift = F32_SUBLANES // 2
    while shift >= 1:
        y = [pltpu.roll(v, shift, 0) for v in x]
        x = [jnp.maximum(x[i], y[n - 1 - i]) for i in range(n)]
        d = n // 2
        while d >= 1:
            for i in range(n):
                if i & d == 0:
                    exchange(i, i + d)
            d //= 2
        shift //= 2
    return [v[0:1, :] for v in x]


def _peer_kernel(h2t_ref, wqt_ref, sk_ref, u_ref, vt_ref, x1_ref, g2_ref, o_ref,
                 rank_sc, p2_sc, cnt_sc, c1_sc, acc_sc):
    j = pl.program_id(1)
    n_steps = pl.num_programs(1)
    nk = PEER_NKEYS
    kk = PEER_TOPK

    @pl.when(j == 0)
    def _route():
        qt = _dot(wqt_ref[...], h2t_ref[...])
        scores = [_dot(sk_ref[h, p], qt[(2 * h + p) * nk:(2 * h + p + 1) * nk, :], precision=HIGHEST)
                  for h in range(PEER_HEADS) for p in range(2)]
        tops = [_top_sorted(sc) for sc in scores]
        assert len(tops[0]) == kk
        cands = [jnp.concatenate([tops[2 * h][r1] + tops[2 * h + 1][r2] for r1 in range(kk) for r2 in range(kk)
                                  if (r1 + 1) * (r2 + 1) <= kk], axis=0) for h in range(PEER_HEADS)]
        bests = _top_values(cands, kk)
        for h in range(PEER_HEADS):
            s1, s2 = scores[2 * h], scores[2 * h + 1]
            a1, a2 = tops[2 * h], tops[2 * h + 1]
            tau = bests[h][kk - 1]
            z = jnp.sum(jnp.where(cands[h] >= tau, jnp.exp(cands[h] - bests[h][0]), 0.0), axis=0, keepdims=True)
            top1 = jnp.concatenate(a1, axis=0)
            per_rank = jnp.zeros_like(top1)
            for r2 in range(kk):
                per_rank = per_rank + jnp.where(top1 + a2[r2] >= tau, 1.0, 0.0)
            cnt = jnp.zeros_like(s1)
            rank2 = jnp.full(s2.shape, float(kk), F32)
            for r in reversed(range(kk)):
                cnt = jnp.where(s1 >= a1[r], per_rank[r:r + 1, :], cnt)
                rank2 = jnp.where(s2 >= a2[r], float(r), rank2)
            e1 = jnp.where(s1 >= a1[kk - 1], jnp.exp(s1 - a1[0]), 0.0)
            e2 = jnp.where(s2 >= a2[kk - 1], jnp.exp(s2 - a2[0]), 0.0)
            rank_sc[h] = rank2.astype(BF16)
            p2_sc[h] = e2.astype(BF16)
            cnt_sc[h] = cnt
            c1_sc[h] = e1 / z
        acc_sc[...] = jnp.zeros_like(acc_sc)

    te = u_ref.shape[0]
    sub = PEER_EXPERT_SUBTILE
    tm = h2t_ref.shape[1]
    pack = 16
    gates = []
    for ai in range(te // nk):
        a = j * (te // nk) + ai
        w = None
        for h in range(PEER_HEADS):
            cnt = jnp.broadcast_to(cnt_sc[h, pl.ds(a, 1), :], (pack, tm)).astype(BF16)
            c1 = jnp.broadcast_to(c1_sc[h, pl.ds(a, 1), :], (pack, tm)).astype(BF16)
            rank2 = rank_sc[h].reshape(nk // pack, pack, tm)
            p2 = p2_sc[h].reshape(nk // pack, pack, tm)
            term = jnp.where(rank2 < cnt[None], p2 * c1[None], jnp.zeros((), BF16))
            w = term if w is None else w + term
        gates.append(w.reshape(nk, tm))
    h2t = h2t_ref[...]
    pre = [_dot(u_ref[si * sub:(si + 1) * sub, :], h2t) for si in range(te // sub)]
    per_sub = sub // nk
    b = jnp.concatenate(
        [jax.nn.gelu(pre[si].astype(BF16)) * jnp.concatenate(gates[si * per_sub:(si + 1) * per_sub], axis=0)
         for si in range(te // sub)], axis=0)
    acc_sc[...] += _dot(vt_ref[...], b)

    @pl.when(j == n_steps - 1)
    def _finish():
        o_ref[...] = x1_ref[...] + g2_ref[0] * acc_sc[...].T


def _peer(h2t, x1, seq, g2, w_q, subkeys, u_tab, v_tab):
    d, n = h2t.shape
    tm = PEER_TOKEN_TILE
    te = PEER_EXPERT_TILE
    tiles_per_seq = seq // tm
    ne = u_tab.shape[0]
    heads = subkeys.shape[0]
    per_key = (heads, PEER_NKEYS, tm)
    scratch = [pltpu.VMEM(per_key, BF16), pltpu.VMEM(per_key, BF16), pltpu.VMEM(per_key, F32),
               pltpu.VMEM(per_key, F32), pltpu.VMEM((d, tm), F32)]
    return pl.pallas_call(
        _peer_kernel,
        grid=(n // tm, ne // te),
        in_specs=[
            pl.BlockSpec((d, tm), lambda i, j: (0, i)),
            pl.BlockSpec((w_q.shape[1], d), lambda i, j: (0, 0)),
            pl.BlockSpec(subkeys.shape, lambda i, j: (0, 0, 0, 0)),
            pl.BlockSpec((te, d), lambda i, j: (j, 0)),
            pl.BlockSpec((d, te), lambda i, j: (0, j)),
            pl.BlockSpec((tm, d), lambda i, j: (i, 0)),
            pl.BlockSpec((1, 1, d), lambda i, j: (i // tiles_per_seq, 0, 0)),
        ],
        out_specs=pl.BlockSpec((tm, d), lambda i, j: (i, 0)),
        out_shape=jax.ShapeDtypeStruct((n, d), F32),
        scratch_shapes=scratch,
        compiler_params=_cparams("arbitrary", "arbitrary"),
        name="peer_ffn",
    )(h2t, w_q.T.astype(BF16), subkeys, u_tab.astype(BF16), v_tab.T.astype(BF16), x1, g2)


def kernel(x, c, w_mod, b_mod, norm_mix, norm_ffn, even_w_in, gmlp_v_gain, gmlp_w_s, gmlp_b_s,
           moba_q_gain, moba_k_gain, even_w_out, odd_w_in, odd_w_out, peer_w_q, peer_subkeys,
           peer_u, peer_v):
    bsz, seq, d = x.shape
    depth = w_mod.shape[0]
    assert seq % TOKEN_TILE == 0 and seq % ATTN_TILE == 0 and seq % PEER_TOKEN_TILE == 0
    mod = _modulation(c, w_mod, b_mod)
    xs = x.reshape(bsz * seq, d)
    for layer in range(depth):
        i = layer // 2
        sh1, sc1, g1, sh2, sc2, g2 = [mod[layer, :, k * d:(k + 1) * d].reshape(bsz, 1, d) for k in range(6)]
        if layer % 2 == 0:
            ya, qt, k, vt, km = _in_proj0(xs, seq, sc1, sh1, norm_mix[layer], even_w_in[i],
                                          gmlp_v_gain[i], gmlp_w_s[i], gmlp_b_s[i],
                                          moba_q_gain[i], moba_k_gain[i])
            yb = _moba(qt, k, vt, km, bsz, seq)
            ys, w_out = [ya, yb], even_w_out[i]
        else:
            qt, k, vt, kabs = _in_proj1(xs, seq, sc1, sh1, norm_mix[layer], odd_w_in[i])
            y = _stick_breaking(qt, k, vt, kabs, bsz, seq)
            ys, w_out = [y], odd_w_out[i]
        x1, h2t = _out_proj(ys, w_out, xs, seq, g1, norm_ffn[layer], sc2, sh2)
        xs = _peer(h2t, x1, seq, g2, peer_w_q[layer], peer_subkeys[layer], peer_u[layer], peer_v[layer])
    return xs.reshape(bsz, seq, d)
```

```python
import functools

import jax
import jax.numpy as jnp
from jax import lax
from jax.experimental import pallas as pl
from jax.experimental.pallas import tpu as pltpu

F32 = jnp.float32
BF16 = jnp.bfloat16
HIGHEST = lax.Precision.HIGHEST

EPS = 1e-6
HEAD_DIM = 64
GMLP_GROUPS = 4
GMLP_CHUNK = 128
MOBA_BLOCK = 256
MOBA_TOPK = 3
ROPE_THETA = 500000.0
ROPE_DIM = HEAD_DIM // 4
PEER_HEADS = 8
PEER_NKEYS = 128
PEER_TOPK = 16
NEG_BIG = -1e30
LOG2E = 1.4426950408889634

VMEM_LIMIT_BYTES = 56 * 1024 * 1024
TOKEN_TILE = 512
ATTN_TILE = 256
HEAD_GROUP = 8
SB_LOG_FLOOR = -110.0
PEER_TOKEN_TILE = 512
PEER_EXPERT_TILE = 2048
PEER_EXPERT_SUBTILE = 256
NT_DIMS = (((1,), (1,)), ((), ()))


def _cparams(*sem):
    return pltpu.CompilerParams(dimension_semantics=sem, vmem_limit_bytes=VMEM_LIMIT_BYTES)


def _dot(a, b, **kw):
    return jnp.dot(a, b, preferred_element_type=F32, **kw)


def _group_sum(x, blockdiag):
    hi = x.astype(BF16)
    lo = (x - hi.astype(F32)).astype(BF16)
    return _dot(hi, blockdiag) + _dot(lo, blockdiag)


def _adaln(x, gain, sc, sh):
    ms = jnp.mean(x * x, axis=-1, keepdims=True)
    return x * lax.rsqrt(ms + EPS) * gain * (1.0 + sc) + sh


def _mod_kernel(c_ref, w_ref, b_ref, o_ref):
    c = c_ref[...]
    ca = c * jax.nn.sigmoid(c)
    o_ref[0] = _dot(ca, w_ref[0], precision=HIGHEST) + b_ref[0]


def _modulation(c, w_mod, b_mod):
    depth, d, d6 = w_mod.shape
    bsz = c.shape[0]
    rows = 8
    c_pad = jnp.zeros((rows, d), F32).at[:bsz].set(c)
    tn = 1536
    out = pl.pallas_call(
        _mod_kernel,
        grid=(depth, d6 // tn),
        in_specs=[
            pl.BlockSpec((rows, d), lambda l, n: (0, 0)),
            pl.BlockSpec((1, d, tn), lambda l, n: (l, 0, n)),
            pl.BlockSpec((1, 1, tn), lambda l, n: (l, 0, n)),
        ],
        out_specs=pl.BlockSpec((1, rows, tn), lambda l, n: (l, 0, n)),
        out_shape=jax.ShapeDtypeStruct((depth, rows, d6), F32),
        compiler_params=_cparams("arbitrary", "arbitrary"),
        name="adaln_modulation",
    )(c_pad, w_mod, b_mod.reshape(depth, 1, d6))
    return out[:, :bsz]


def _in0_kernel(x_ref, sc_ref, sh_ref, gn_ref, w_ref, wvt_ref, vgain_ref, ws_ref, bs_ref, qg_ref, kg_ref,
                cos_ref, sina_ref, sinb_ref, bd128_ref, bd64_ref,
                ya_ref, qt_ref, k_ref, vt_ref, km_ref):
    tm = x_ref.shape[0]
    gw = GMLP_GROUPS * GMLP_CHUNK
    h = _adaln(x_ref[...], gn_ref[...], sc_ref[0], sh_ref[0]).astype(BF16)
    proj = _dot(h, w_ref[...])
    ua, va = proj[:, :gw], proj[:, gw:2 * gw]
    qb, kb = proj[:, 2 * gw:3 * gw], proj[:, 3 * gw:4 * gw]
    vt = lax.dot_general(wvt_ref[...], h, NT_DIMS, preferred_element_type=F32).astype(BF16)
    for blk in range(tm // MOBA_BLOCK):
        vt_ref[blk] = vt[:, blk * MOBA_BLOCK:(blk + 1) * MOBA_BLOCK]

    u = jax.nn.gelu(ua)
    gv = jax.nn.gelu(va)
    ss = _group_sum(gv * gv, bd128_ref[...]) * (1.0 / GMLP_CHUNK)
    vn = (gv * lax.rsqrt(ss + EPS) * vgain_ref[...]).astype(BF16)
    r = lax.broadcasted_iota(jnp.int32, (GMLP_CHUNK, GMLP_CHUNK), 0)
    c = lax.broadcasted_iota(jnp.int32, (GMLP_CHUNK, GMLP_CHUNK), 1)
    for g in range(GMLP_GROUPS):
        wg = jnp.where(c <= r, ws_ref[g], 0.0).astype(BF16)
        lanes = slice(g * GMLP_CHUNK, (g + 1) * GMLP_CHUNK)
        for ch in range(tm // GMLP_CHUNK):
            rows = slice(ch * GMLP_CHUNK, (ch + 1) * GMLP_CHUNK)
            mixed = _dot(wg, vn[rows, lanes]) + bs_ref[g]
            ya_ref[rows, lanes] = (u[rows, lanes] * mixed).astype(BF16)

    reps = gw // cos_ref.shape[1]
    cos = jnp.concatenate([cos_ref[...]] * reps, axis=1)
    sina = jnp.concatenate([sina_ref[...]] * reps, axis=1)
    sinb = jnp.concatenate([sinb_ref[...]] * reps, axis=1)
    half = ROPE_DIM // 2

    def norm_rope(t, gain):
        ms = _group_sum(t * t, bd64_ref[...]) * (1.0 / HEAD_DIM)
        tn = t * lax.rsqrt(ms + EPS) * gain
        return tn * cos + pltpu.roll(tn, gw - half, 1) * sina + pltpu.roll(tn, half, 1) * sinb

    q = norm_rope(qb, qg_ref[...]) * (HEAD_DIM ** -0.5 * LOG2E)
    k = norm_rope(kb, kg_ref[...])
    qt_ref[...] = q.T
    k_ref[...] = k.astype(BF16)
    km_ref[0] = jnp.mean(k.reshape(tm // MOBA_BLOCK, MOBA_BLOCK, gw), axis=1)


def _rope_tables(seq):
    half = ROPE_DIM // 2
    inv_freq = ROPE_THETA ** (-jnp.arange(half, dtype=F32) / half)
    ang = jnp.arange(seq, dtype=F32)[:, None] * inv_freq[None, :]
    lane = jnp.arange(2 * HEAD_DIM) % HEAD_DIM
    cos_l = jnp.cos(ang)[:, lane % half]
    sin_l = jnp.sin(ang)[:, lane % half]
    cos = jnp.where(lane < ROPE_DIM, cos_l, 1.0)
    sina = jnp.where(lane < half, -sin_l, 0.0)
    sinb = jnp.where((lane >= half) & (lane < ROPE_DIM), sin_l, 0.0)
    return cos, sina, sinb


def _blockdiag(n, group):
    idx = jnp.arange(n) // group
    return (idx[:, None] == idx[None, :]).astype(BF16)


def _in_proj0(x2d, seq, sc, sh, gain, w_in, v_gain, w_s, b_s, q_gain, k_gain):
    n, d = x2d.shape
    tm = TOKEN_TILE
    tiles_per_seq = seq // tm
    gw = GMLP_GROUPS * GMLP_CHUNK
    n_in = w_in.shape[1]
    cos, sina, sinb = _rope_tables(seq)
    heads = gw // HEAD_DIM
    row_spec = lambda w: pl.BlockSpec((tm, w), lambda i: (i, 0))
    const2 = lambda a: pl.BlockSpec(a.shape, lambda i: (0,) * a.ndim)
    mod_spec = pl.BlockSpec((1, 1, d), lambda i: (i // tiles_per_seq, 0, 0))
    rope_spec = pl.BlockSpec((tm, 2 * HEAD_DIM), lambda i: (i % tiles_per_seq, 0))
    assert n_in == 5 * gw
    w_bf = w_in.astype(BF16)
    args = [x2d, sc, sh, gain.reshape(1, d), w_bf[:, :4 * gw], w_bf[:, 4 * gw:].T, v_gain.reshape(1, gw), w_s,
            jnp.broadcast_to(b_s[:, :, None], b_s.shape + (GMLP_CHUNK,)),
            jnp.tile(q_gain, heads).reshape(1, gw), jnp.tile(k_gain, heads).reshape(1, gw),
            cos, sina, sinb, _blockdiag(gw, GMLP_CHUNK), _blockdiag(gw, HEAD_DIM)]
    in_specs = [row_spec(d), mod_spec, mod_spec] + [const2(a) for a in args[3:11]] \
        + [rope_spec] * 3 + [const2(a) for a in args[14:]]
    blocks = tm // MOBA_BLOCK
    ya, qt, k, vt, km = pl.pallas_call(
        _in0_kernel,
        grid=(n // tm,),
        in_specs=in_specs,
        out_specs=[row_spec(gw), pl.BlockSpec((gw, tm), lambda i: (0, i)), row_spec(gw),
                   pl.BlockSpec((blocks, gw, MOBA_BLOCK), lambda i: (i, 0, 0)),
                   pl.BlockSpec((1, blocks, gw), lambda i: (i, 0, 0))],
        out_shape=[jax.ShapeDtypeStruct((n, gw), BF16), jax.ShapeDtypeStruct((gw, n), F32),
                   jax.ShapeDtypeStruct((n, gw), BF16), jax.ShapeDtypeStruct((n // MOBA_BLOCK, gw, MOBA_BLOCK), BF16),
                   jax.ShapeDtypeStruct((n // tm, blocks, gw), F32)],
        compiler_params=_cparams("arbitrary"),
        name="layer0_in_proj",
    )(*args)
    return ya, qt, k, vt, km.reshape(n // MOBA_BLOCK, gw)


def _pair_rows(ref_rows, h):
    rows = lax.broadcasted_iota(jnp.int32, ref_rows.shape, 0)
    mine = rows < HEAD_DIM if h % 2 == 0 else rows >= HEAD_DIM
    return jnp.where(mine, ref_rows, 0.0)


def _moba_kernel(qt_ref, k_ref, vt_ref, km_ref, o_ref, *scratch):
    i = pl.program_id(2)
    t = ATTN_TILE
    pw = 2 * HEAD_DIM
    heads = qt_ref.shape[0] // HEAD_DIM
    sel_sc, m_sc, l_sc, acc_sc = (scratch[n * heads:(n + 1) * heads] for n in range(4))
    nb = km_ref.shape[0]
    key = lax.broadcasted_iota(jnp.int32, (t, t), 0)
    qry = lax.broadcasted_iota(jnp.int32, (t, t), 1)
    blk = lax.broadcasted_iota(jnp.int32, (nb, t), 0)
    past = blk < i

    qm = []
    for h in range(heads):
        feats = slice((h // 2) * pw, (h // 2 + 1) * pw)
        qh = _pair_rows(qt_ref[feats, :], h)
        qm.append(qh.astype(BF16))
        gate = _dot(km_ref[:, feats], qh, precision=HIGHEST)
        g = jnp.where(past, gate, -jnp.inf)
        kth = None
        for _ in range(MOBA_TOPK):
            kth = jnp.max(g, axis=0, keepdims=True)
            g = jnp.where(g >= kth, -jnp.inf, g)
        sel_sc[h][...] = jnp.where(past & (gate >= kth), 1.0, 0.0)
        m_sc[h][...] = jnp.full_like(m_sc[h], NEG_BIG)
        l_sc[h][...] = jnp.zeros_like(l_sc[h])
        acc_sc[h][...] = jnp.zeros_like(acc_sc[h])

    def update(jb, diag):
        start = pl.multiple_of(jb * t, t)
        pair_feats = [slice(p * pw, (p + 1) * pw) for p in range(heads // 2)]
        scores = [_dot(k_ref[0, pl.ds(start, t), pair_feats[h // 2]], qm[h]) for h in range(heads)]
        probs, alphas = [], []
        for h in range(heads):
            keep = (key <= qry) if diag else (sel_sc[h][pl.ds(jb, 1), :] > 0.5)
            s = jnp.where(keep, scores[h], NEG_BIG)
            m_old = m_sc[h][...]
            m_new = jnp.maximum(m_old, jnp.max(s, axis=0, keepdims=True))
            alpha = jnp.exp2(m_old - m_new)
            p = jnp.exp2(s - m_new)
            l_sc[h][...] = alpha * l_sc[h][...] + jnp.sum(p, axis=0, keepdims=True)
            m_sc[h][...] = m_new
            probs.append(p.astype(BF16))
            alphas.append(alpha)
        for h in range(heads):
            acc_sc[h][...] = alphas[h] * acc_sc[h][...] + _dot(vt_ref[jb, pair_feats[h // 2], :], probs[h])

    def past_block(jb, carry):
        update(jb, False)
        return carry

    lax.fori_loop(0, i, past_block, 0)
    update(i, True)
    top = lax.broadcasted_iota(jnp.int32, (pw, t), 0) < HEAD_DIM
    for p in range(heads // 2):
        out = jnp.where(top, acc_sc[2 * p][...] / l_sc[2 * p][...], acc_sc[2 * p + 1][...] / l_sc[2 * p + 1][...])
        o_ref[:, p * pw:(p + 1) * pw] = out.T.astype(o_ref.dtype)


def _moba(qt, k, vt, kmean, bsz, seq):
    width, n = qt.shape
    t = ATTN_TILE
    nb = seq // MOBA_BLOCK
    assert t == MOBA_BLOCK
    gw = HEAD_GROUP * HEAD_DIM
    nq = seq // t
    return pl.pallas_call(
        _moba_kernel,
        grid=(bsz, width // gw, nq),
        in_specs=[pl.BlockSpec((gw, t), lambda b, g, i: (g, b * nq + i)),
                  pl.BlockSpec((1, seq, gw), lambda b, g, i: (b, 0, g)),
                  pl.BlockSpec((nb, gw, t), lambda b, g, i: (b, g, 0)),
                  pl.BlockSpec((nb, gw), lambda b, g, i: (b, g))],
        out_specs=pl.BlockSpec((t, gw), lambda b, g, i: (b * nq + i, g)),
        out_shape=jax.ShapeDtypeStruct((n, width), BF16),
        scratch_shapes=[pltpu.VMEM(shape, F32) for shape in ((nb, t), (1, t), (1, t), (2 * HEAD_DIM, t))
                        for _ in range(HEAD_GROUP)],
        compiler_params=_cparams("arbitrary", "arbitrary", "arbitrary"),
        name="moba_attention",
    )(qt, k.reshape(bsz, seq, width), vt, kmean)


def _sb_kernel(qt_ref, k_ref, vt_ref, kabs_ref, tri_ref, o_ref, *scratch):
    i = pl.program_id(2)
    t = ATTN_TILE
    pw = 2 * HEAD_DIM
    heads = qt_ref.shape[0] // HEAD_DIM
    run_sc, acc_sc = scratch[:heads], scratch[heads:]
    tri = tri_ref[...]
    key = lax.broadcasted_iota(jnp.int32, (t, t), 0)
    qry = lax.broadcasted_iota(jnp.int32, (t, t), 1)
    strictly_past = key < qry
    kabs = kabs_ref[0]

    qm, zmax = [], []
    for h in range(heads):
        feats = slice((h // 2) * pw, (h // 2 + 1) * pw)
        qh = _pair_rows(qt_ref[feats, :].astype(F32), h)
        qm.append(qh.astype(BF16))
        kmax = jnp.max(kabs[:, h * HEAD_DIM:(h + 1) * HEAD_DIM], axis=1, keepdims=True)
        zmax.append(jnp.sum(jnp.abs(qh), axis=0, keepdims=True) * kmax)
        run_sc[h][...] = jnp.zeros_like(run_sc[h])
        acc_sc[h][...] = jnp.zeros_like(acc_sc[h])

    def block(jb, diag):
        start = pl.multiple_of(jb * t, t)
        pair_feats = [slice(p * pw, (p + 1) * pw) for p in range(heads // 2)]
        zs = [_dot(k_ref[0, pl.ds(start, t), pair_feats[h // 2]], qm[h]) for h in range(heads)]
        loms = []
        for h in range(heads):
            lom = -(jnp.maximum(zs[h], 0.0) + jnp.log(1.0 + jnp.exp(-jnp.abs(zs[h]))))
            if diag:
                lom = jnp.where(strictly_past, lom, 0.0)
            loms.append(lom.astype(BF16))
        suffixes = [_dot(tri, loms[h]) for h in range(heads)]
        weights, worst = [], None
        for h in range(heads):
            run = run_sc[h][...]
            a = jnp.exp(zs[h] + suffixes[h] + run)
            if diag:
                a = jnp.where(strictly_past, a, 0.0)
            weights.append(a.astype(BF16))
            run = run + suffixes[h][0:1, :]
            run_sc[h][...] = run
            reach = run + zmax[h]
            worst = reach if worst is None else jnp.maximum(worst, reach)
        for h in range(heads):
            acc_sc[h][...] += _dot(vt_ref[jb, pair_feats[h // 2], :], weights[h])
        return jnp.max(worst)

    def more(c):
        n, worst = c
        return jnp.logical_and(n < i, worst > SB_LOG_FLOOR)

    lax.while_loop(more, lambda c: (c[0] + 1, block(i - 1 - c[0], False)), (jnp.int32(0), block(i, True)))
    top = lax.broadcasted_iota(jnp.int32, (pw, t), 0) < HEAD_DIM
    for p in range(heads // 2):
        out = jnp.where(top, acc_sc[2 * p][...], acc_sc[2 * p + 1][...])
        o_ref[:, p * pw:(p + 1) * pw] = out.T.astype(o_ref.dtype)


def _stick_breaking(qt, k, vt, kabs, bsz, seq):
    width, n = qt.shape
    t = ATTN_TILE
    gw = HEAD_GROUP * HEAD_DIM
    nq = seq // t
    idx = jnp.arange(t)
    tri = (idx[None, :] >= idx[:, None]).astype(BF16)
    return pl.pallas_call(
        _sb_kernel,
        grid=(bsz, width // gw, nq),
        in_specs=[pl.BlockSpec((gw, t), lambda b, g, i: (g, b * nq + i)),
                  pl.BlockSpec((1, seq, gw), lambda b, g, i: (b, 0, g)),
                  pl.BlockSpec((nq, gw, t), lambda b, g, i: (b, g, 0)),
                  pl.BlockSpec((1, 1, gw), lambda b, g, i: (b, 0, g)),
                  pl.BlockSpec((t, t), lambda b, g, i: (0, 0))],
        out_specs=pl.BlockSpec((t, gw), lambda b, g, i: (b * nq + i, g)),
        out_shape=jax.ShapeDtypeStruct((n, width), BF16),
        scratch_shapes=[pltpu.VMEM(shape, F32) for shape in ((1, t), (2 * HEAD_DIM, t)) for _ in range(HEAD_GROUP)],
        compiler_params=_cparams("arbitrary", "arbitrary", "arbitrary"),
        name="stick_breaking_attention",
    )(qt, k.reshape(bsz, seq, width), vt, kabs, tri)


def _in1_kernel(x_ref, sc_ref, sh_ref, gn_ref, wk_ref, wqt_ref, wvt_ref, qt_ref, k_ref, vt_ref, kabs_ref,
                *, tiles_per_seq):
    i = pl.program_id(0)
    tm = x_ref.shape[0]
    h = _adaln(x_ref[...], gn_ref[...], sc_ref[0], sh_ref[0]).astype(BF16)
    k = _dot(h, wk_ref[...]).astype(BF16)
    k_ref[...] = k
    qt = lax.dot_general(wqt_ref[...], h, NT_DIMS, preferred_element_type=F32)
    qt_ref[...] = (qt * (HEAD_DIM ** -0.5)).astype(BF16)
    vt = lax.dot_general(wvt_ref[...], h, NT_DIMS, preferred_element_type=F32).astype(BF16)
    for blk in range(tm // ATTN_TILE):
        vt_ref[blk] = vt[:, blk * ATTN_TILE:(blk + 1) * ATTN_TILE]
    kabs = jnp.max(jnp.abs(k.astype(F32)), axis=0, keepdims=True)

    @pl.when(i % tiles_per_seq == 0)
    def _first():
        kabs_ref[0] = kabs

    @pl.when(i % tiles_per_seq != 0)
    def _rest():
        kabs_ref[0] = jnp.maximum(kabs_ref[0], kabs)


def _in_proj1(x2d, seq, sc, sh, gain, w_in):
    n, d = x2d.shape
    tm = TOKEN_TILE
    tiles_per_seq = seq // tm
    width = w_in.shape[1] // 3
    blocks = tm // ATTN_TILE
    w_bf = w_in.astype(BF16)
    row_spec = lambda w: pl.BlockSpec((tm, w), lambda i: (i, 0))
    mod_spec = pl.BlockSpec((1, 1, d), lambda i: (i // tiles_per_seq, 0, 0))
    w_spec = pl.BlockSpec((width, d), lambda i: (0, 0))
    return pl.pallas_call(
        functools.partial(_in1_kernel, tiles_per_seq=tiles_per_seq),
        grid=(n // tm,),
        in_specs=[row_spec(d), mod_spec, mod_spec, pl.BlockSpec((1, d), lambda i: (0, 0)),
                  pl.BlockSpec((d, width), lambda i: (0, 0)), w_spec, w_spec],
        out_specs=[pl.BlockSpec((width, tm), lambda i: (0, i)), row_spec(width),
                   pl.BlockSpec((blocks, width, ATTN_TILE), lambda i: (i, 0, 0)),
                   pl.BlockSpec((1, 1, width), lambda i: (i // tiles_per_seq, 0, 0))],
        out_shape=[jax.ShapeDtypeStruct((width, n), BF16), jax.ShapeDtypeStruct((n, width), BF16),
                   jax.ShapeDtypeStruct((n // ATTN_TILE, width, ATTN_TILE), BF16),
                   jax.ShapeDtypeStruct((n // seq, 1, width), F32)],
        compiler_params=_cparams("arbitrary"),
        name="layer1_in_proj",
    )(x2d, sc, sh, gain.reshape(1, d), w_bf[:, width:2 * width], w_bf[:, :width].T, w_bf[:, 2 * width:].T)


def _out_kernel(*refs, widths):
    n_in = len(widths)
    y_refs = refs[:n_in]
    w_ref, x_ref, g1_ref, gn_ref, sc_ref, sh_ref, x1_ref, h2t_ref = refs[n_in:]
    y = None
    off = 0
    for y_ref, wd in zip(y_refs, widths):
        part = _dot(y_ref[...], w_ref[off:off + wd, :])
        y = part if y is None else y + part
        off += wd
    x1 = x_ref[...] + g1_ref[0] * y
    x1_ref[...] = x1
    h2 = _adaln(x1, gn_ref[...], sc_ref[0], sh_ref[0])
    h2t_ref[...] = h2.T.astype(BF16)


def _out_proj(ys, w_out, x2d, seq, g1, gain, sc, sh):
    n, d = x2d.shape
    tm = TOKEN_TILE
    tiles_per_seq = seq // tm
    widths = tuple(y.shape[1] for y in ys)
    row_spec = lambda w: pl.BlockSpec((tm, w), lambda i: (i, 0))
    mod_spec = pl.BlockSpec((1, 1, d), lambda i: (i // tiles_per_seq, 0, 0))
    return pl.pallas_call(
        functools.partial(_out_kernel, widths=widths),
        grid=(n // tm,),
        in_specs=[row_spec(w) for w in widths]
        + [pl.BlockSpec(w_out.shape, lambda i: (0, 0)), row_spec(d), mod_spec,
           pl.BlockSpec((1, d), lambda i: (0, 0)), mod_spec, mod_spec],
        out_specs=[row_spec(d), pl.BlockSpec((d, tm), lambda i: (0, i))],
        out_shape=[jax.ShapeDtypeStruct((n, d), F32), jax.ShapeDtypeStruct((d, n), BF16)],
        compiler_params=_cparams("arbitrary"),
        name="out_proj_residual",
    )(*ys, w_out.astype(BF16), x2d, g1, gain.reshape(1, d), sc, sh)


def _top_values(arrays, n):
    arrays = list(arrays)
    vals = [[] for _ in arrays]
    for _ in range(n):
        for idx, s in enumerate(arrays):
            m = jnp.max(s, axis=0, keepdims=True)
            vals[idx].append(m)
            arrays[idx] = jnp.where(s >= m, -jnp.inf, s)
    return vals


def _oddeven_merge(lo, hi, r):
    step = r * 2
    if step < hi - lo:
        yield from _oddeven_merge(lo, hi, step)
        yield from _oddeven_merge(lo + r, hi, step)
        for i in range(lo + r, hi - r, step):
            yield (i, i + r)
    else:
        yield (lo, lo + r)


def _oddeven_merge_sort(lo, hi):
    if hi - lo >= 1:
        mid = lo + (hi - lo) // 2
        yield from _oddeven_merge_sort(lo, mid)
        yield from _oddeven_merge_sort(mid + 1, hi)
        yield from _oddeven_merge(lo, hi, 1)


F32_SUBLANES = 8


def _top_sorted(s):
    n = s.shape[0] // F32_SUBLANES
    x = [s[F32_SUBLANES * i:F32_SUBLANES * (i + 1), :] for i in range(n)]

    def exchange(i, j):
        x[i], x[j] = jnp.maximum(x[i], x[j]), jnp.minimum(x[i], x[j])

    for i, j in _oddeven_merge_sort(0, n - 1):
        exchange(i, j)
    shift = F32_SUBLANES // 2
    while shift >= 1:
        y = [pltpu.roll(v, shift, 0) for v in x]
        x = [jnp.maximum(x[i], y[n - 1 - i]) for i in range(n)]
        d = n // 2
        while d >= 1:
            for i in range(n):
                if i & d == 0:
                    exchange(i, i + d)
            d //= 2
        shift //= 2
    return [v[0:1, :] for v in x]


def _peer_kernel(h2t_ref, wqt_ref, sk_ref, u_ref, vt_ref, x1_ref, g2_ref, o_ref,
                 rank_sc, p2_sc, cnt_sc, c1_sc, acc_sc):
    j = pl.program_id(1)
    n_steps = pl.num_programs(1)
    nk = PEER_NKEYS
    kk = PEER_TOPK

    @pl.when(j == 0)
    def _route():
        qt = _dot(wqt_ref[...], h2t_ref[...])
        scores = [_dot(sk_ref[h, p], qt[(2 * h + p) * nk:(2 * h + p + 1) * nk, :], precision=HIGHEST)
                  for h in range(PEER_HEADS) for p in range(2)]
        tops = [_top_sorted(sc) for sc in scores]
        assert len(tops[0]) == kk
        cands = [jnp.concatenate([tops[2 * h][r1] + tops[2 * h + 1][r2] for r1 in range(kk) for r2 in range(kk)
                                  if (r1 + 1) * (r2 + 1) <= kk], axis=0) for h in range(PEER_HEADS)]
        bests = _top_values(cands, kk)
        for h in range(PEER_HEADS):
            s1, s2 = scores[2 * h], scores[2 * h + 1]
            a1, a2 = tops[2 * h], tops[2 * h + 1]
            tau = bests[h][kk - 1]
            z = jnp.sum(jnp.where(cands[h] >= tau, jnp.exp(cands[h] - bests[h][0]), 0.0), axis=0, keepdims=True)
            top1 = jnp.concatenate(a1, axis=0)
            per_rank = jnp.zeros_like(top1)
            for r2 in range(kk):
                per_rank = per_rank + jnp.where(top1 + a2[r2] >= tau, 1.0, 0.0)
            cnt = jnp.zeros_like(s1)
            rank2 = jnp.full(s2.shape, float(kk), F32)
            for r in reversed(range(kk)):
                cnt = jnp.where(s1 >= a1[r], per_rank[r:r + 1, :], cnt)
                rank2 = jnp.where(s2 >= a2[r], float(r), rank2)
            e1 = jnp.where(s1 >= a1[kk - 1], jnp.exp(s1 - a1[0]), 0.0)
            e2 = jnp.where(s2 >= a2[kk - 1], jnp.exp(s2 - a2[0]), 0.0)
            rank_sc[h] = rank2.astype(BF16)
            p2_sc[h] = e2.astype(BF16)
            cnt_sc[h] = cnt
            c1_sc[h] = e1 / z
        acc_sc[...] = jnp.zeros_like(acc_sc)

    te = u_ref.shape[0]
    sub = PEER_EXPERT_SUBTILE
    tm = h2t_ref.shape[1]
    pack = 16
    gates = []
    for ai in range(te // nk):
        a = j * (te // nk) + ai
        w = None
        for h in range(PEER_HEADS):
            cnt = jnp.broadcast_to(cnt_sc[h, pl.ds(a, 1), :], (pack, tm)).astype(BF16)
            c1 = jnp.broadcast_to(c1_sc[h, pl.ds(a, 1), :], (pack, tm)).astype(BF16)
            rank2 = rank_sc[h].reshape(nk // pack, pack, tm)
            p2 = p2_sc[h].reshape(nk // pack, pack, tm)
            term = jnp.where(rank2 < cnt[None], p2 * c1[None], jnp.zeros((), BF16))
            w = term if w is None else w + term
        gates.append(w.reshape(nk, tm))
    h2t = h2t_ref[...]
    pre = [_dot(u_ref[si * sub:(si + 1) * sub, :], h2t) for si in range(te // sub)]
    per_sub = sub // nk
    b = jnp.concatenate(
        [jax.nn.gelu(pre[si].astype(BF16)) * jnp.concatenate(gates[si * per_sub:(si + 1) * per_sub], axis=0)
         for si in range(te // sub)], axis=0)
    acc_sc[...] += _dot(vt_ref[...], b)

    @pl.when(j == n_steps - 1)
    def _finish():
        o_ref[...] = x1_ref[...] + g2_ref[0] * acc_sc[...].T


def _peer(h2t, x1, seq, g2, layer, wq_t, subkeys, u_tab, vt_tab):
    d, n = h2t.shape
    tm = PEER_TOKEN_TILE
    te = PEER_EXPERT_TILE
    tiles_per_seq = seq // tm
    ne = u_tab.shape[1]
    heads = subkeys.shape[1]
    per_key = (heads, PEER_NKEYS, tm)
    scratch = [pltpu.VMEM(per_key, BF16), pltpu.VMEM(per_key, BF16), pltpu.VMEM(per_key, F32),
               pltpu.VMEM(per_key, F32), pltpu.VMEM((d, tm), F32)]
    return pl.pallas_call(
        _peer_kernel,
        grid=(n // tm, ne // te),
        in_specs=[
            pl.BlockSpec((d, tm), lambda i, j: (0, i)),
            pl.BlockSpec((None,) + wq_t.shape[1:], lambda i, j: (layer, 0, 0)),
            pl.BlockSpec((None,) + subkeys.shape[1:], lambda i, j: (layer, 0, 0, 0, 0)),
            pl.BlockSpec((None, te, d), lambda i, j: (layer, j, 0)),
            pl.BlockSpec((None, d, te), lambda i, j: (layer, 0, j)),
            pl.BlockSpec((tm, d), lambda i, j: (i, 0)),
            pl.BlockSpec((1, 1, d), lambda i, j: (i // tiles_per_seq, 0, 0)),
        ],
        out_specs=pl.BlockSpec((tm, d), lambda i, j: (i, 0)),
        out_shape=jax.ShapeDtypeStruct((n, d), F32),
        scratch_shapes=scratch,
        compiler_params=_cparams("arbitrary", "arbitrary"),
        name="peer_ffn",
    )(h2t, wq_t, subkeys, u_tab, vt_tab, x1, g2)


def kernel(x, c, w_mod, b_mod, norm_mix, norm_ffn, even_w_in, gmlp_v_gain, gmlp_w_s, gmlp_b_s,
           moba_q_gain, moba_k_gain, even_w_out, odd_w_in, odd_w_out, peer_w_q, peer_subkeys,
           peer_u, peer_v):
    bsz, seq, d = x.shape
    depth = w_mod.shape[0]
    assert seq % TOKEN_TILE == 0 and seq % ATTN_TILE == 0 and seq % PEER_TOKEN_TILE == 0
    mod = _modulation(c, w_mod, b_mod)
    xs = x.reshape(bsz * seq, d)
    wq_t = peer_w_q.transpose(0, 2, 1).astype(BF16)
    u_bf = peer_u.astype(BF16)
    vt_bf = peer_v.transpose(0, 2, 1).astype(BF16)
    for layer in range(depth):
        i = layer // 2
        sh1, sc1, g1, sh2, sc2, g2 = [mod[layer, :, k * d:(k + 1) * d].reshape(bsz, 1, d) for k in range(6)]
        if layer % 2 == 0:
            ya, qt, k, vt, km = _in_proj0(xs, seq, sc1, sh1, norm_mix[layer], even_w_in[i],
                                          gmlp_v_gain[i], gmlp_w_s[i], gmlp_b_s[i],
                                          moba_q_gain[i], moba_k_gain[i])
            yb = _moba(qt, k, vt, km, bsz, seq)
            ys, w_out = [ya, yb], even_w_out[i]
        else:
            qt, k, vt, kabs = _in_proj1(xs, seq, sc1, sh1, norm_mix[layer], odd_w_in[i])
            y = _stick_breaking(qt, k, vt, kabs, bsz, seq)
            ys, w_out = [y], odd_w_out[i]
        x1, h2t = _out_proj(ys, w_out, xs, seq, g1, norm_ffn[layer], sc2, sh2)
        xs = _peer(h2t, x1, seq, g2, layer, wq_t, peer_subkeys, u_bf, vt_bf)
    return xs.reshape(bsz, seq, d)
```

```python
import functools

import jax
import jax.numpy as jnp
from jax import lax
from jax.experimental import pallas as pl
from jax.experimental.pallas import tpu as pltpu

F32 = jnp.float32
BF16 = jnp.bfloat16
HIGHEST = lax.Precision.HIGHEST

EPS = 1e-6
HEAD_DIM = 64
GMLP_GROUPS = 4
GMLP_CHUNK = 128
MOBA_BLOCK = 256
MOBA_TOPK = 3
ROPE_THETA = 500000.0
ROPE_DIM = HEAD_DIM // 4
PEER_HEADS = 8
PEER_NKEYS = 128
PEER_TOPK = 16
NEG_BIG = -1e30
LOG2E = 1.4426950408889634

VMEM_LIMIT_BYTES = 56 * 1024 * 1024
TOKEN_TILE = 512
ATTN_TILE = 256
HEAD_GROUP = 8
SB_LOG2_FLOOR = -160.0
F32_SUBLANES = 8
BF16_SUBLANES = 16
PEER_TOKEN_TILE = 512
PEER_EXPERT_TILE = 2048
PEER_EXPERT_SUBTILE = 256
NT_DIMS = (((1,), (1,)), ((), ()))


def _cparams(*sem):
    return pltpu.CompilerParams(dimension_semantics=sem, vmem_limit_bytes=VMEM_LIMIT_BYTES)


def _dot(a, b, **kw):
    return jnp.dot(a, b, preferred_element_type=F32, **kw)


def _group_sum(x, blockdiag):
    hi = x.astype(BF16)
    lo = (x - hi.astype(F32)).astype(BF16)
    return _dot(hi, blockdiag) + _dot(lo, blockdiag)


def _adaln(x, gain, sc, sh):
    ms = jnp.mean(x * x, axis=-1, keepdims=True)
    return x * lax.rsqrt(ms + EPS) * gain * (1.0 + sc) + sh


def _mod_kernel(c_ref, w_ref, b_ref, o_ref):
    c = c_ref[...]
    ca = c * jax.nn.sigmoid(c)
    o_ref[0] = _dot(ca, w_ref[0], precision=HIGHEST) + b_ref[0]


def _modulation(c, w_mod, b_mod):
    depth, d, d6 = w_mod.shape
    bsz = c.shape[0]
    rows = F32_SUBLANES
    c_pad = jnp.zeros((rows, d), F32).at[:bsz].set(c)
    tn = 1536
    out = pl.pallas_call(
        _mod_kernel,
        grid=(depth, d6 // tn),
        in_specs=[
            pl.BlockSpec((rows, d), lambda l, n: (0, 0)),
            pl.BlockSpec((1, d, tn), lambda l, n: (l, 0, n)),
            pl.BlockSpec((1, 1, tn), lambda l, n: (l, 0, n)),
        ],
        out_specs=pl.BlockSpec((1, rows, tn), lambda l, n: (l, 0, n)),
        out_shape=jax.ShapeDtypeStruct((depth, rows, d6), F32),
        compiler_params=_cparams("arbitrary", "arbitrary"),
        name="adaln_modulation",
    )(c_pad, w_mod, b_mod.reshape(depth, 1, d6))
    return out[:, :bsz]


def _in0_kernel(x_ref, sc_ref, sh_ref, gn_ref, w_ref, wvt_ref, vgain_ref, ws_ref, bs_ref, qg_ref, kg_ref,
                cos_ref, sina_ref, sinb_ref, bd128_ref, bd64_ref,
                ya_ref, qt_ref, k_ref, vt_ref, km_ref):
    tm = x_ref.shape[0]
    gw = GMLP_GROUPS * GMLP_CHUNK
    h = _adaln(x_ref[...], gn_ref[...], sc_ref[0], sh_ref[0]).astype(BF16)
    proj = _dot(h, w_ref[...])
    ua, va = proj[:, :gw], proj[:, gw:2 * gw]
    qb, kb = proj[:, 2 * gw:3 * gw], proj[:, 3 * gw:4 * gw]
    vt = lax.dot_general(wvt_ref[...], h, NT_DIMS, preferred_element_type=F32).astype(BF16)
    for blk in range(tm // MOBA_BLOCK):
        vt_ref[blk] = vt[:, blk * MOBA_BLOCK:(blk + 1) * MOBA_BLOCK]

    u = jax.nn.gelu(ua)
    gv = jax.nn.gelu(va)
    ss = _group_sum(gv * gv, bd128_ref[...]) * (1.0 / GMLP_CHUNK)
    vn = (gv * lax.rsqrt(ss + EPS) * vgain_ref[...]).astype(BF16)
    r = lax.broadcasted_iota(jnp.int32, (GMLP_CHUNK, GMLP_CHUNK), 0)
    c = lax.broadcasted_iota(jnp.int32, (GMLP_CHUNK, GMLP_CHUNK), 1)
    for g in range(GMLP_GROUPS):
        wg = jnp.where(c <= r, ws_ref[g], 0.0).astype(BF16)
        lanes = slice(g * GMLP_CHUNK, (g + 1) * GMLP_CHUNK)
        for ch in range(tm // GMLP_CHUNK):
            rows = slice(ch * GMLP_CHUNK, (ch + 1) * GMLP_CHUNK)
            mixed = _dot(wg, vn[rows, lanes]) + bs_ref[g]
            ya_ref[rows, lanes] = (u[rows, lanes] * mixed).astype(BF16)

    reps = gw // cos_ref.shape[1]
    cos = jnp.concatenate([cos_ref[...]] * reps, axis=1)
    sina = jnp.concatenate([sina_ref[...]] * reps, axis=1)
    sinb = jnp.concatenate([sinb_ref[...]] * reps, axis=1)
    half = ROPE_DIM // 2

    def norm_rope(t, gain):
        ms = _group_sum(t * t, bd64_ref[...]) * (1.0 / HEAD_DIM)
        tn = t * lax.rsqrt(ms + EPS) * gain
        return tn * cos + pltpu.roll(tn, gw - half, 1) * sina + pltpu.roll(tn, half, 1) * sinb

    q = norm_rope(qb, qg_ref[...]) * (HEAD_DIM ** -0.5 * LOG2E)
    k = norm_rope(kb, kg_ref[...])
    qt_ref[...] = q.T
    k_ref[...] = k.astype(BF16)
    km_ref[0] = jnp.mean(k.reshape(tm // MOBA_BLOCK, MOBA_BLOCK, gw), axis=1)


def _rope_tables(seq):
    half = ROPE_DIM // 2
    inv_freq = ROPE_THETA ** (-jnp.arange(half, dtype=F32) / half)
    ang = jnp.arange(seq, dtype=F32)[:, None] * inv_freq[None, :]
    lane = jnp.arange(2 * HEAD_DIM) % HEAD_DIM
    cos_l = jnp.cos(ang)[:, lane % half]
    sin_l = jnp.sin(ang)[:, lane % half]
    cos = jnp.where(lane < ROPE_DIM, cos_l, 1.0)
    sina = jnp.where(lane < half, -sin_l, 0.0)
    sinb = jnp.where((lane >= half) & (lane < ROPE_DIM), sin_l, 0.0)
    return cos, sina, sinb


def _blockdiag(n, group):
    idx = jnp.arange(n) // group
    return (idx[:, None] == idx[None, :]).astype(BF16)


def _in_proj0(x2d, seq, sc, sh, gain, w_in, v_gain, w_s, b_s, q_gain, k_gain):
    n, d = x2d.shape
    tm = TOKEN_TILE
    tiles_per_seq = seq // tm
    gw = GMLP_GROUPS * GMLP_CHUNK
    n_in = w_in.shape[1]
    cos, sina, sinb = _rope_tables(seq)
    heads = gw // HEAD_DIM
    row_spec = lambda w: pl.BlockSpec((tm, w), lambda i: (i, 0))
    const2 = lambda a: pl.BlockSpec(a.shape, lambda i: (0,) * a.ndim)
    mod_spec = pl.BlockSpec((1, 1, d), lambda i: (i // tiles_per_seq, 0, 0))
    rope_spec = pl.BlockSpec((tm, 2 * HEAD_DIM), lambda i: (i % tiles_per_seq, 0))
    assert n_in == 5 * gw
    w_bf = w_in.astype(BF16)
    args = [x2d, sc, sh, gain.reshape(1, d), w_bf[:, :4 * gw], w_bf[:, 4 * gw:].T, v_gain.reshape(1, gw), w_s,
            jnp.broadcast_to(b_s[:, :, None], b_s.shape + (GMLP_CHUNK,)),
            jnp.tile(q_gain, heads).reshape(1, gw), jnp.tile(k_gain, heads).reshape(1, gw),
            cos, sina, sinb, _blockdiag(gw, GMLP_CHUNK), _blockdiag(gw, HEAD_DIM)]
    in_specs = [row_spec(d), mod_spec, mod_spec] + [const2(a) for a in args[3:11]] \
        + [rope_spec] * 3 + [const2(a) for a in args[14:]]
    blocks = tm // MOBA_BLOCK
    ya, qt, k, vt, km = pl.pallas_call(
        _in0_kernel,
        grid=(n // tm,),
        in_specs=in_specs,
        out_specs=[row_spec(gw), pl.BlockSpec((gw, tm), lambda i: (0, i)), row_spec(gw),
                   pl.BlockSpec((blocks, gw, MOBA_BLOCK), lambda i: (i, 0, 0)),
                   pl.BlockSpec((1, blocks, gw), lambda i: (i, 0, 0))],
        out_shape=[jax.ShapeDtypeStruct((n, gw), BF16), jax.ShapeDtypeStruct((gw, n), F32),
                   jax.ShapeDtypeStruct((n, gw), BF16), jax.ShapeDtypeStruct((n // MOBA_BLOCK, gw, MOBA_BLOCK), BF16),
                   jax.ShapeDtypeStruct((n // tm, blocks, gw), F32)],
        compiler_params=_cparams("arbitrary"),
        name="layer0_in_proj",
    )(*args)
    return ya, qt, k, vt, km.reshape(n // MOBA_BLOCK, gw)


def _pair_rows(ref_rows, h):
    rows = lax.broadcasted_iota(jnp.int32, ref_rows.shape, 0)
    mine = rows < HEAD_DIM if h % 2 == 0 else rows >= HEAD_DIM
    return jnp.where(mine, ref_rows, 0.0)


def _moba_kernel(qt_ref, k_ref, vt_ref, km_ref, o_ref, *scratch):
    i = pl.program_id(2)
    t = ATTN_TILE
    pw = 2 * HEAD_DIM
    heads = qt_ref.shape[0] // HEAD_DIM
    sel_sc, m_sc, l_sc, acc_sc = (scratch[n * heads:(n + 1) * heads] for n in range(4))
    nb = km_ref.shape[0]
    key = lax.broadcasted_iota(jnp.int32, (t, t), 0)
    qry = lax.broadcasted_iota(jnp.int32, (t, t), 1)
    blk = lax.broadcasted_iota(jnp.int32, (nb, t), 0)
    past = blk < i

    qm = []
    for h in range(heads):
        feats = slice((h // 2) * pw, (h // 2 + 1) * pw)
        qh = _pair_rows(qt_ref[feats, :], h)
        qm.append(qh.astype(BF16))
        gate = _dot(km_ref[:, feats], qh, precision=HIGHEST)
        g = jnp.where(past, gate, -jnp.inf)
        kth = None
        for _ in range(MOBA_TOPK):
            kth = jnp.max(g, axis=0, keepdims=True)
            g = jnp.where(g >= kth, -jnp.inf, g)
        sel_sc[h][...] = jnp.where(past & (gate >= kth), 1.0, 0.0)
        m_sc[h][...] = jnp.full_like(m_sc[h], NEG_BIG)
        l_sc[h][...] = jnp.zeros_like(l_sc[h])
        acc_sc[h][...] = jnp.zeros_like(acc_sc[h])

    def update(jb, diag):
        start = pl.multiple_of(jb * t, t)
        pair_feats = [slice(p * pw, (p + 1) * pw) for p in range(heads // 2)]
        scores = [_dot(k_ref[0, pl.ds(start, t), pair_feats[h // 2]], qm[h]) for h in range(heads)]
        probs, alphas = [], []
        for h in range(heads):
            keep = (key <= qry) if diag else (sel_sc[h][pl.ds(jb, 1), :] > 0.5)
            s = jnp.where(keep, scores[h], NEG_BIG)
            m_old = m_sc[h][...]
            m_new = jnp.maximum(m_old, jnp.max(s, axis=0, keepdims=True))
            alpha = jnp.exp2(m_old - m_new)
            p = jnp.exp2(s - m_new)
            l_sc[h][...] = alpha * l_sc[h][...] + jnp.sum(p, axis=0, keepdims=True)
            m_sc[h][...] = m_new
            probs.append(p.astype(BF16))
            alphas.append(alpha)
        for h in range(heads):
            acc_sc[h][...] = alphas[h] * acc_sc[h][...] + _dot(vt_ref[jb, pair_feats[h // 2], :], probs[h])

    def past_block(jb, carry):
        update(jb, False)
        return carry

    lax.fori_loop(0, i, past_block, 0)
    update(i, True)
    top = lax.broadcasted_iota(jnp.int32, (pw, t), 0) < HEAD_DIM
    for p in range(heads // 2):
        out = jnp.where(top, acc_sc[2 * p][...] / l_sc[2 * p][...], acc_sc[2 * p + 1][...] / l_sc[2 * p + 1][...])
        o_ref[:, p * pw:(p + 1) * pw] = out.T.astype(o_ref.dtype)


def _moba(qt, k, vt, kmean, bsz, seq):
    width, n = qt.shape
    t = ATTN_TILE
    nb = seq // MOBA_BLOCK
    assert t == MOBA_BLOCK
    gw = HEAD_GROUP * HEAD_DIM
    nq = seq // t
    return pl.pallas_call(
        _moba_kernel,
        grid=(bsz, width // gw, nq),
        in_specs=[pl.BlockSpec((gw, t), lambda b, g, i: (g, b * nq + i)),
                  pl.BlockSpec((1, seq, gw), lambda b, g, i: (b, 0, g)),
                  pl.BlockSpec((nb, gw, t), lambda b, g, i: (b, g, 0)),
                  pl.BlockSpec((nb, gw), lambda b, g, i: (b, g))],
        out_specs=pl.BlockSpec((t, gw), lambda b, g, i: (b * nq + i, g)),
        out_shape=jax.ShapeDtypeStruct((n, width), BF16),
        scratch_shapes=[pltpu.VMEM(shape, F32) for shape in ((nb, t), (1, t), (1, t), (2 * HEAD_DIM, t))
                        for _ in range(HEAD_GROUP)],
        compiler_params=_cparams("arbitrary", "arbitrary", "arbitrary"),
        name="moba_attention",
    )(qt, k.reshape(bsz, seq, width), vt, kmean)


def _sb_kernel(qt_ref, k_ref, vt_ref, kabs_ref, tri_ref, o_ref, *scratch):
    i = pl.program_id(2)
    t = ATTN_TILE
    pw = 2 * HEAD_DIM
    heads = qt_ref.shape[0] // HEAD_DIM
    run_sc, acc_sc = scratch[:heads], scratch[heads:]
    tri = tri_ref[...]
    key = lax.broadcasted_iota(jnp.int32, (t, t), 0)
    qry = lax.broadcasted_iota(jnp.int32, (t, t), 1)
    strictly_past = key < qry
    kabs = kabs_ref[0]

    qm, zmax = [], []
    for h in range(heads):
        feats = slice((h // 2) * pw, (h // 2 + 1) * pw)
        qh = _pair_rows(qt_ref[feats, :].astype(F32), h)
        qm.append(qh.astype(BF16))
        kmax = jnp.max(kabs[:, h * HEAD_DIM:(h + 1) * HEAD_DIM], axis=1, keepdims=True)
        zmax.append(jnp.sum(jnp.abs(qh), axis=0, keepdims=True) * kmax)
        run_sc[h][...] = jnp.zeros_like(run_sc[h])
        acc_sc[h][...] = jnp.zeros_like(acc_sc[h])

    def block(jb, diag):
        start = pl.multiple_of(jb * t, t)
        pair_feats = [slice(p * pw, (p + 1) * pw) for p in range(heads // 2)]
        zs = [_dot(k_ref[0, pl.ds(start, t), pair_feats[h // 2]], qm[h]) for h in range(heads)]
        loms = []
        for h in range(heads):
            lom = -(jnp.maximum(zs[h], 0.0) + jnp.log2(1.0 + jnp.exp2(-jnp.abs(zs[h]))))
            if diag:
                lom = jnp.where(strictly_past, lom, 0.0)
            loms.append(lom.astype(BF16))
        suffixes = [_dot(tri, loms[h]) for h in range(heads)]
        weights, worst = [], None
        for h in range(heads):
            run = run_sc[h][...]
            a = jnp.exp2(zs[h] + suffixes[h] + run)
            if diag:
                a = jnp.where(strictly_past, a, 0.0)
            weights.append(a.astype(BF16))
            run = run + suffixes[h][0:1, :]
            run_sc[h][...] = run
            reach = run + zmax[h]
            worst = reach if worst is None else jnp.maximum(worst, reach)
        for h in range(heads):
            acc_sc[h][...] += _dot(vt_ref[jb, pair_feats[h // 2], :], weights[h])
        return jnp.max(worst)

    def more(c):
        n, worst = c
        return jnp.logical_and(n < i, worst > SB_LOG2_FLOOR)

    lax.while_loop(more, lambda c: (c[0] + 1, block(i - 1 - c[0], False)), (jnp.int32(0), block(i, True)))
    top = lax.broadcasted_iota(jnp.int32, (pw, t), 0) < HEAD_DIM
    for p in range(heads // 2):
        out = jnp.where(top, acc_sc[2 * p][...], acc_sc[2 * p + 1][...])
        o_ref[:, p * pw:(p + 1) * pw] = out.T.astype(o_ref.dtype)


def _stick_breaking(qt, k, vt, kabs, bsz, seq):
    width, n = qt.shape
    t = ATTN_TILE
    gw = HEAD_GROUP * HEAD_DIM
    nq = seq // t
    idx = jnp.arange(t)
    tri = (idx[None, :] >= idx[:, None]).astype(BF16)
    return pl.pallas_call(
        _sb_kernel,
        grid=(bsz, width // gw, nq),
        in_specs=[pl.BlockSpec((gw, t), lambda b, g, i: (g, b * nq + i)),
                  pl.BlockSpec((1, seq, gw), lambda b, g, i: (b, 0, g)),
                  pl.BlockSpec((nq, gw, t), lambda b, g, i: (b, g, 0)),
                  pl.BlockSpec((1, 1, gw), lambda b, g, i: (b, 0, g)),
                  pl.BlockSpec((t, t), lambda b, g, i: (0, 0))],
        out_specs=pl.BlockSpec((t, gw), lambda b, g, i: (b * nq + i, g)),
        out_shape=jax.ShapeDtypeStruct((n, width), BF16),
        scratch_shapes=[pltpu.VMEM(shape, F32) for shape in ((1, t), (2 * HEAD_DIM, t)) for _ in range(HEAD_GROUP)],
        compiler_params=_cparams("arbitrary", "arbitrary", "arbitrary"),
        name="stick_breaking_attention",
    )(qt, k.reshape(bsz, seq, width), vt, kabs, tri)


def _in1_kernel(x_ref, sc_ref, sh_ref, gn_ref, wk_ref, wqt_ref, wvt_ref, qt_ref, k_ref, vt_ref, kabs_ref,
                *, tiles_per_seq):
    i = pl.program_id(0)
    tm = x_ref.shape[0]
    h = _adaln(x_ref[...], gn_ref[...], sc_ref[0], sh_ref[0]).astype(BF16)
    k = _dot(h, wk_ref[...]).astype(BF16)
    k_ref[...] = k
    qt = lax.dot_general(wqt_ref[...], h, NT_DIMS, preferred_element_type=F32)
    qt_ref[...] = (qt * (HEAD_DIM ** -0.5 * LOG2E)).astype(BF16)
    vt = lax.dot_general(wvt_ref[...], h, NT_DIMS, preferred_element_type=F32).astype(BF16)
    for blk in range(tm // ATTN_TILE):
        vt_ref[blk] = vt[:, blk * ATTN_TILE:(blk + 1) * ATTN_TILE]
    kabs = jnp.max(jnp.abs(k.astype(F32)), axis=0, keepdims=True)

    @pl.when(i % tiles_per_seq == 0)
    def _first():
        kabs_ref[0] = kabs

    @pl.when(i % tiles_per_seq != 0)
    def _rest():
        kabs_ref[0] = jnp.maximum(kabs_ref[0], kabs)


def _in_proj1(x2d, seq, sc, sh, gain, w_in):
    n, d = x2d.shape
    tm = TOKEN_TILE
    tiles_per_seq = seq // tm
    width = w_in.shape[1] // 3
    blocks = tm // ATTN_TILE
    w_bf = w_in.astype(BF16)
    row_spec = lambda w: pl.BlockSpec((tm, w), lambda i: (i, 0))
    mod_spec = pl.BlockSpec((1, 1, d), lambda i: (i // tiles_per_seq, 0, 0))
    w_spec = pl.BlockSpec((width, d), lambda i: (0, 0))
    return pl.pallas_call(
        functools.partial(_in1_kernel, tiles_per_seq=tiles_per_seq),
        grid=(n // tm,),
        in_specs=[row_spec(d), mod_spec, mod_spec, pl.BlockSpec((1, d), lambda i: (0, 0)),
                  pl.BlockSpec((d, width), lambda i: (0, 0)), w_spec, w_spec],
        out_specs=[pl.BlockSpec((width, tm), lambda i: (0, i)), row_spec(width),
                   pl.BlockSpec((blocks, width, ATTN_TILE), lambda i: (i, 0, 0)),
                   pl.BlockSpec((1, 1, width), lambda i: (i // tiles_per_seq, 0, 0))],
        out_shape=[jax.ShapeDtypeStruct((width, n), BF16), jax.ShapeDtypeStruct((n, width), BF16),
                   jax.ShapeDtypeStruct((n // ATTN_TILE, width, ATTN_TILE), BF16),
                   jax.ShapeDtypeStruct((n // seq, 1, width), F32)],
        compiler_params=_cparams("arbitrary"),
        name="layer1_in_proj",
    )(x2d, sc, sh, gain.reshape(1, d), w_bf[:, width:2 * width], w_bf[:, :width].T, w_bf[:, 2 * width:].T)


def _out_kernel(*refs, widths):
    n_in = len(widths)
    y_refs = refs[:n_in]
    w_ref, x_ref, g1_ref, gn_ref, sc_ref, sh_ref, x1_ref, h2t_ref = refs[n_in:]
    y = None
    off = 0
    for y_ref, wd in zip(y_refs, widths):
        part = _dot(y_ref[...], w_ref[off:off + wd, :])
        y = part if y is None else y + part
        off += wd
    x1 = x_ref[...] + g1_ref[0] * y
    x1_ref[...] = x1
    h2 = _adaln(x1, gn_ref[...], sc_ref[0], sh_ref[0])
    h2t_ref[...] = h2.T.astype(BF16)


def _out_proj(ys, w_out, x2d, seq, g1, gain, sc, sh):
    n, d = x2d.shape
    tm = TOKEN_TILE
    tiles_per_seq = seq // tm
    widths = tuple(y.shape[1] for y in ys)
    row_spec = lambda w: pl.BlockSpec((tm, w), lambda i: (i, 0))
    mod_spec = pl.BlockSpec((1, 1, d), lambda i: (i // tiles_per_seq, 0, 0))
    return pl.pallas_call(
        functools.partial(_out_kernel, widths=widths),
        grid=(n // tm,),
        in_specs=[row_spec(w) for w in widths]
        + [pl.BlockSpec(w_out.shape, lambda i: (0, 0)), row_spec(d), mod_spec,
           pl.BlockSpec((1, d), lambda i: (0, 0)), mod_spec, mod_spec],
        out_specs=[row_spec(d), pl.BlockSpec((d, tm), lambda i: (0, i))],
        out_shape=[jax.ShapeDtypeStruct((n, d), F32), jax.ShapeDtypeStruct((d, n), BF16)],
        compiler_params=_cparams("arbitrary"),
        name="out_proj_residual",
    )(*ys, w_out.astype(BF16), x2d, g1, gain.reshape(1, d), sc, sh)


def _top_values(arrays, n):
    arrays = list(arrays)
    vals = [[] for _ in arrays]
    for _ in range(n):
        for idx, s in enumerate(arrays):
            m = jnp.max(s, axis=0, keepdims=True)
            vals[idx].append(m)
            arrays[idx] = jnp.where(s >= m, -jnp.inf, s)
    return vals


def _oddeven_merge(lo, hi, r):
    step = r * 2
    if step < hi - lo:
        yield from _oddeven_merge(lo, hi, step)
        yield from _oddeven_merge(lo + r, hi, step)
        for i in range(lo + r, hi - r, step):
            yield (i, i + r)
    else:
        yield (lo, lo + r)


def _oddeven_merge_sort(lo, hi):
    if hi - lo >= 1:
        mid = lo + (hi - lo) // 2
        yield from _oddeven_merge_sort(lo, mid)
        yield from _oddeven_merge_sort(mid + 1, hi)
        yield from _oddeven_merge(lo, hi, 1)


def _top_sorted(s):
    n = s.shape[0] // F32_SUBLANES
    x = [s[F32_SUBLANES * i:F32_SUBLANES * (i + 1), :] for i in range(n)]

    def exchange(i, j):
        x[i], x[j] = jnp.maximum(x[i], x[j]), jnp.minimum(x[i], x[j])

    for i, j in _oddeven_merge_sort(0, n - 1):
        exchange(i, j)
    shift = F32_SUBLANES // 2
    while shift >= 1:
        y = [pltpu.roll(v, shift, 0) for v in x]
        x = [jnp.maximum(x[i], y[n - 1 - i]) for i in range(n)]
        d = n // 2
        while d >= 1:
            for i in range(n):
                if i & d == 0:
                    exchange(i, i + d)
            d //= 2
        shift //= 2
    return [v[0:1, :] for v in x]


def _peer_kernel(h2t_ref, wqt_ref, sk_ref, u_ref, vt_ref, x1_ref, g2_ref, o_ref,
                 rank_sc, p2_sc, cnt_sc, c1_sc, acc_sc):
    j = pl.program_id(1)
    n_steps = pl.num_programs(1)
    nk = PEER_NKEYS
    kk = PEER_TOPK

    @pl.when(j == 0)
    def _route():
        qt = _dot(wqt_ref[...], h2t_ref[...])
        scores = [_dot(sk_ref[h, p], qt[(2 * h + p) * nk:(2 * h + p + 1) * nk, :], precision=HIGHEST)
                  for h in range(PEER_HEADS) for p in range(2)]
        tops = [_top_sorted(sc) for sc in scores]
        assert len(tops[0]) == kk
        cands = [jnp.concatenate([tops[2 * h][r1] + tops[2 * h + 1][r2] for r1 in range(kk) for r2 in range(kk)
                                  if (r1 + 1) * (r2 + 1) <= kk], axis=0) for h in range(PEER_HEADS)]
        bests = _top_values(cands, kk)
        for h in range(PEER_HEADS):
            s1, s2 = scores[2 * h], scores[2 * h + 1]
            a1, a2 = tops[2 * h], tops[2 * h + 1]
            tau = bests[h][kk - 1]
            z = jnp.sum(jnp.where(cands[h] >= tau, jnp.exp(cands[h] - bests[h][0]), 0.0), axis=0, keepdims=True)
            top1 = jnp.concatenate(a1, axis=0)
            per_rank = jnp.zeros_like(top1)
            for r2 in range(kk):
                per_rank = per_rank + jnp.where(top1 + a2[r2] >= tau, 1.0, 0.0)
            cnt = jnp.zeros_like(s1)
            rank2 = jnp.full(s2.shape, float(kk), F32)
            for r in reversed(range(kk)):
                cnt = jnp.where(s1 >= a1[r], per_rank[r:r + 1, :], cnt)
                rank2 = jnp.where(s2 >= a2[r], float(r), rank2)
            e1 = jnp.where(s1 >= a1[kk - 1], jnp.exp(s1 - a1[0]), 0.0)
            e2 = jnp.where(s2 >= a2[kk - 1], jnp.exp(s2 - a2[0]), 0.0)
            rank_sc[h] = rank2.astype(BF16)
            p2_sc[h] = e2.astype(BF16)
            cnt_sc[h] = cnt
            c1_sc[h] = e1 / z
        acc_sc[...] = jnp.zeros_like(acc_sc)

    te = u_ref.shape[0]
    sub = PEER_EXPERT_SUBTILE
    tm = h2t_ref.shape[1]
    pack = BF16_SUBLANES
    gates = []
    for ai in range(te // nk):
        a = j * (te // nk) + ai
        w = None
        for h in range(PEER_HEADS):
            cnt = jnp.broadcast_to(cnt_sc[h, pl.ds(a, 1), :], (pack, tm)).astype(BF16)
            c1 = jnp.broadcast_to(c1_sc[h, pl.ds(a, 1), :], (pack, tm)).astype(BF16)
            rank2 = rank_sc[h].reshape(nk // pack, pack, tm)
            p2 = p2_sc[h].reshape(nk // pack, pack, tm)
            term = jnp.where(rank2 < cnt[None], p2 * c1[None], jnp.zeros((), BF16))
            w = term if w is None else w + term
        gates.append(w.reshape(nk, tm))
    h2t = h2t_ref[...]
    pre = [_dot(u_ref[si * sub:(si + 1) * sub, :], h2t) for si in range(te // sub)]
    per_sub = sub // nk
    b = jnp.concatenate(
        [jax.nn.gelu(pre[si].astype(BF16)) * jnp.concatenate(gates[si * per_sub:(si + 1) * per_sub], axis=0)
         for si in range(te // sub)], axis=0)
    acc_sc[...] += _dot(vt_ref[...], b)

    @pl.when(j == n_steps - 1)
    def _finish():
        o_ref[...] = x1_ref[...] + g2_ref[0] * acc_sc[...].T


def _peer(h2t, x1, seq, g2, layer, wq_t, subkeys, u_tab, vt_tab):
    d, n = h2t.shape
    tm = PEER_TOKEN_TILE
    te = PEER_EXPERT_TILE
    tiles_per_seq = seq // tm
    ne = u_tab.shape[1]
    heads = subkeys.shape[1]
    per_key = (heads, PEER_NKEYS, tm)
    scratch = [pltpu.VMEM(per_key, BF16), pltpu.VMEM(per_key, BF16), pltpu.VMEM(per_key, F32),
               pltpu.VMEM(per_key, F32), pltpu.VMEM((d, tm), F32)]
    return pl.pallas_call(
        _peer_kernel,
        grid=(n // tm, ne // te),
        in_specs=[
            pl.BlockSpec((d, tm), lambda i, j: (0, i)),
            pl.BlockSpec((None,) + wq_t.shape[1:], lambda i, j: (layer, 0, 0)),
            pl.BlockSpec((None,) + subkeys.shape[1:], lambda i, j: (layer, 0, 0, 0, 0)),
            pl.BlockSpec((None, te, d), lambda i, j: (layer, j, 0)),
            pl.BlockSpec((None, d, te), lambda i, j: (layer, 0, j)),
            pl.BlockSpec((tm, d), lambda i, j: (i, 0)),
            pl.BlockSpec((1, 1, d), lambda i, j: (i // tiles_per_seq, 0, 0)),
        ],
        out_specs=pl.BlockSpec((tm, d), lambda i, j: (i, 0)),
        out_shape=jax.ShapeDtypeStruct((n, d), F32),
        scratch_shapes=scratch,
        compiler_params=_cparams("arbitrary", "arbitrary"),
        name="peer_ffn",
    )(h2t, wq_t, subkeys, u_tab, vt_tab, x1, g2)


def kernel(x, c, w_mod, b_mod, norm_mix, norm_ffn, even_w_in, gmlp_v_gain, gmlp_w_s, gmlp_b_s,
           moba_q_gain, moba_k_gain, even_w_out, odd_w_in, odd_w_out, peer_w_q, peer_subkeys,
           peer_u, peer_v):
    bsz, seq, d = x.shape
    depth = w_mod.shape[0]
    assert seq % TOKEN_TILE == 0 and seq % ATTN_TILE == 0 and seq % PEER_TOKEN_TILE == 0
    mod = _modulation(c, w_mod, b_mod)
    xs = x.reshape(bsz * seq, d)
    wq_t = peer_w_q.transpose(0, 2, 1).astype(BF16)
    u_bf = peer_u.astype(BF16)
    vt_bf = peer_v.transpose(0, 2, 1).astype(BF16)
    for layer in range(depth):
        i = layer // 2
        sh1, sc1, g1, sh2, sc2, g2 = [mod[layer, :, k * d:(k + 1) * d].reshape(bsz, 1, d) for k in range(6)]
        if layer % 2 == 0:
            ya, qt, k, vt, km = _in_proj0(xs, seq, sc1, sh1, norm_mix[layer], even_w_in[i],
                                          gmlp_v_gain[i], gmlp_w_s[i], gmlp_b_s[i],
                                          moba_q_gain[i], moba_k_gain[i])
            yb = _moba(qt, k, vt, km, bsz, seq)
            ys, w_out = [ya, yb], even_w_out[i]
        else:
            qt, k, vt, kabs = _in_proj1(xs, seq, sc1, sh1, norm_mix[layer], odd_w_in[i])
            y = _stick_breaking(qt, k, vt, kabs, bsz, seq)
            ys, w_out = [y], odd_w_out[i]
        x1, h2t = _out_proj(ys, w_out, xs, seq, g1, norm_ffn[layer], sc2, sh2)
        xs = _peer(h2t, x1, seq, g2, layer, wq_t, peer_subkeys, u_bf, vt_bf)
    return xs.reshape(bsz, seq, d)
```

```python
import functools

import jax
import jax.numpy as jnp
from jax import lax
from jax.experimental import pallas as pl
from jax.experimental.pallas import tpu as pltpu

F32 = jnp.float32
BF16 = jnp.bfloat16
HIGHEST = lax.Precision.HIGHEST

EPS = 1e-6
HEAD_DIM = 64
GMLP_GROUPS = 4
GMLP_CHUNK = 128
MOBA_BLOCK = 256
MOBA_TOPK = 3
ROPE_THETA = 500000.0
ROPE_DIM = HEAD_DIM // 4
PEER_HEADS = 8
PEER_NKEYS = 128
PEER_TOPK = 16
NEG_BIG = -1e30
LOG2E = 1.4426950408889634

VMEM_LIMIT_BYTES = 56 * 1024 * 1024
TOKEN_TILE = 512
ATTN_TILE = 256
HEAD_GROUP = 8
SB_HEAD_GROUP = 4
SB_LOG_FLOOR = -110.0
PEER_TOKEN_TILE = 512
PEER_EXPERT_TILE = 2048
PEER_EXPERT_SUBTILE = 256
NT_DIMS = (((1,), (1,)), ((), ()))


def _cparams(*sem):
    return pltpu.CompilerParams(dimension_semantics=sem, vmem_limit_bytes=VMEM_LIMIT_BYTES)


def _dot(a, b, **kw):
    return jnp.dot(a, b, preferred_element_type=F32, **kw)


def _group_sum(x, blockdiag):
    hi = x.astype(BF16)
    lo = (x - hi.astype(F32)).astype(BF16)
    return _dot(hi, blockdiag) + _dot(lo, blockdiag)


def _adaln(x, gain, sc, sh):
    ms = jnp.mean(x * x, axis=-1, keepdims=True)
    return x * lax.rsqrt(ms + EPS) * gain * (1.0 + sc) + sh


def _mod_kernel(c_ref, w_ref, b_ref, o_ref):
    c = c_ref[...]
    ca = c * jax.nn.sigmoid(c)
    o_ref[0] = _dot(ca, w_ref[0], precision=HIGHEST) + b_ref[0]


def _modulation(c, w_mod, b_mod):
    depth, d, d6 = w_mod.shape
    bsz = c.shape[0]
    rows = 8
    c_pad = jnp.zeros((rows, d), F32).at[:bsz].set(c)
    tn = 1536
    out = pl.pallas_call(
        _mod_kernel,
        grid=(depth, d6 // tn),
        in_specs=[
            pl.BlockSpec((rows, d), lambda l, n: (0, 0)),
            pl.BlockSpec((1, d, tn), lambda l, n: (l, 0, n)),
            pl.BlockSpec((1, 1, tn), lambda l, n: (l, 0, n)),
        ],
        out_specs=pl.BlockSpec((1, rows, tn), lambda l, n: (l, 0, n)),
        out_shape=jax.ShapeDtypeStruct((depth, rows, d6), F32),
        compiler_params=_cparams("arbitrary", "arbitrary"),
        name="adaln_modulation",
    )(c_pad, w_mod, b_mod.reshape(depth, 1, d6))
    return out[:, :bsz]


def _in0_kernel(x_ref, sc_ref, sh_ref, gn_ref, w_ref, wvt_ref, vgain_ref, ws_ref, bs_ref, qg_ref, kg_ref,
                cos_ref, sina_ref, sinb_ref, bd128_ref, bd64_ref,
                ya_ref, qt_ref, k_ref, vt_ref, km_ref):
    tm = x_ref.shape[0]
    gw = GMLP_GROUPS * GMLP_CHUNK
    h = _adaln(x_ref[...], gn_ref[...], sc_ref[0], sh_ref[0]).astype(BF16)
    proj = _dot(h, w_ref[...])
    ua, va = proj[:, :gw], proj[:, gw:2 * gw]
    qb, kb = proj[:, 2 * gw:3 * gw], proj[:, 3 * gw:4 * gw]
    vt = lax.dot_general(wvt_ref[...], h, NT_DIMS, preferred_element_type=F32).astype(BF16)
    for blk in range(tm // MOBA_BLOCK):
        vt_ref[blk] = vt[:, blk * MOBA_BLOCK:(blk + 1) * MOBA_BLOCK]

    u = jax.nn.gelu(ua)
    gv = jax.nn.gelu(va)
    ss = _group_sum(gv * gv, bd128_ref[...]) * (1.0 / GMLP_CHUNK)
    vn = (gv * lax.rsqrt(ss + EPS) * vgain_ref[...]).astype(BF16)
    r = lax.broadcasted_iota(jnp.int32, (GMLP_CHUNK, GMLP_CHUNK), 0)
    c = lax.broadcasted_iota(jnp.int32, (GMLP_CHUNK, GMLP_CHUNK), 1)
    for g in range(GMLP_GROUPS):
        wg = jnp.where(c <= r, ws_ref[g], 0.0).astype(BF16)
        lanes = slice(g * GMLP_CHUNK, (g + 1) * GMLP_CHUNK)
        for ch in range(tm // GMLP_CHUNK):
            rows = slice(ch * GMLP_CHUNK, (ch + 1) * GMLP_CHUNK)
            mixed = _dot(wg, vn[rows, lanes]) + bs_ref[g]
            ya_ref[rows, lanes] = (u[rows, lanes] * mixed).astype(BF16)

    reps = gw // cos_ref.shape[1]
    cos = jnp.concatenate([cos_ref[...]] * reps, axis=1)
    sina = jnp.concatenate([sina_ref[...]] * reps, axis=1)
    sinb = jnp.concatenate([sinb_ref[...]] * reps, axis=1)
    half = ROPE_DIM // 2

    def norm_rope(t, gain):
        ms = _group_sum(t * t, bd64_ref[...]) * (1.0 / HEAD_DIM)
        tn = t * lax.rsqrt(ms + EPS) * gain
        return tn * cos + pltpu.roll(tn, gw - half, 1) * sina + pltpu.roll(tn, half, 1) * sinb

    q = norm_rope(qb, qg_ref[...]) * (HEAD_DIM ** -0.5 * LOG2E)
    k = norm_rope(kb, kg_ref[...])
    qt_ref[...] = q.T
    k_ref[...] = k.astype(BF16)
    km_ref[0] = jnp.mean(k.reshape(tm // MOBA_BLOCK, MOBA_BLOCK, gw), axis=1)


def _rope_tables(seq):
    half = ROPE_DIM // 2
    inv_freq = ROPE_THETA ** (-jnp.arange(half, dtype=F32) / half)
    ang = jnp.arange(seq, dtype=F32)[:, None] * inv_freq[None, :]
    lane = jnp.arange(2 * HEAD_DIM) % HEAD_DIM
    cos_l = jnp.cos(ang)[:, lane % half]
    sin_l = jnp.sin(ang)[:, lane % half]
    cos = jnp.where(lane < ROPE_DIM, cos_l, 1.0)
    sina = jnp.where(lane < half, -sin_l, 0.0)
    sinb = jnp.where((lane >= half) & (lane < ROPE_DIM), sin_l, 0.0)
    return cos, sina, sinb


def _blockdiag(n, group):
    idx = jnp.arange(n) // group
    return (idx[:, None] == idx[None, :]).astype(BF16)


def _in_proj0(x2d, seq, sc, sh, gain, w_in, v_gain, w_s, b_s, q_gain, k_gain):
    n, d = x2d.shape
    tm = TOKEN_TILE
    tiles_per_seq = seq // tm
    gw = GMLP_GROUPS * GMLP_CHUNK
    n_in = w_in.shape[1]
    cos, sina, sinb = _rope_tables(seq)
    heads = gw // HEAD_DIM
    row_spec = lambda w: pl.BlockSpec((tm, w), lambda i: (i, 0))
    const2 = lambda a: pl.BlockSpec(a.shape, lambda i: (0,) * a.ndim)
    mod_spec = pl.BlockSpec((1, 1, d), lambda i: (i // tiles_per_seq, 0, 0))
    rope_spec = pl.BlockSpec((tm, 2 * HEAD_DIM), lambda i: (i % tiles_per_seq, 0))
    assert n_in == 5 * gw
    w_bf = w_in.astype(BF16)
    args = [x2d, sc, sh, gain.reshape(1, d), w_bf[:, :4 * gw], w_bf[:, 4 * gw:].T, v_gain.reshape(1, gw), w_s,
            jnp.broadcast_to(b_s[:, :, None], b_s.shape + (GMLP_CHUNK,)),
            jnp.tile(q_gain, heads).reshape(1, gw), jnp.tile(k_gain, heads).reshape(1, gw),
            cos, sina, sinb, _blockdiag(gw, GMLP_CHUNK), _blockdiag(gw, HEAD_DIM)]
    in_specs = [row_spec(d), mod_spec, mod_spec] + [const2(a) for a in args[3:11]] \
        + [rope_spec] * 3 + [const2(a) for a in args[14:]]
    blocks = tm // MOBA_BLOCK
    ya, qt, k, vt, km = pl.pallas_call(
        _in0_kernel,
        grid=(n // tm,),
        in_specs=in_specs,
        out_specs=[row_spec(gw), pl.BlockSpec((gw, tm), lambda i: (0, i)), row_spec(gw),
                   pl.BlockSpec((blocks, gw, MOBA_BLOCK), lambda i: (i, 0, 0)),
                   pl.BlockSpec((1, blocks, gw), lambda i: (i, 0, 0))],
        out_shape=[jax.ShapeDtypeStruct((n, gw), BF16), jax.ShapeDtypeStruct((gw, n), F32),
                   jax.ShapeDtypeStruct((n, gw), BF16), jax.ShapeDtypeStruct((n // MOBA_BLOCK, gw, MOBA_BLOCK), BF16),
                   jax.ShapeDtypeStruct((n // tm, blocks, gw), F32)],
        compiler_params=_cparams("arbitrary"),
        name="layer0_in_proj",
    )(*args)
    return ya, qt, k, vt, km.reshape(n // MOBA_BLOCK, gw)


def _pair_rows(ref_rows, h):
    rows = lax.broadcasted_iota(jnp.int32, ref_rows.shape, 0)
    mine = rows < HEAD_DIM if h % 2 == 0 else rows >= HEAD_DIM
    return jnp.where(mine, ref_rows, 0.0)


def _moba_kernel(qt_ref, k_ref, vt_ref, km_ref, o_ref, *scratch):
    i = pl.program_id(2)
    t = ATTN_TILE
    pw = 2 * HEAD_DIM
    heads = qt_ref.shape[0] // HEAD_DIM
    sel_sc, m_sc, l_sc, acc_sc = (scratch[n * heads:(n + 1) * heads] for n in range(4))
    nb = km_ref.shape[0]
    key = lax.broadcasted_iota(jnp.int32, (t, t), 0)
    qry = lax.broadcasted_iota(jnp.int32, (t, t), 1)
    blk = lax.broadcasted_iota(jnp.int32, (nb, t), 0)
    past = blk < i

    qm = []
    for h in range(heads):
        feats = slice((h // 2) * pw, (h // 2 + 1) * pw)
        qh = _pair_rows(qt_ref[feats, :], h)
        qm.append(qh.astype(BF16))
        gate = _dot(km_ref[:, feats], qh, precision=HIGHEST)
        g = jnp.where(past, gate, -jnp.inf)
        kth = None
        for _ in range(MOBA_TOPK):
            kth = jnp.max(g, axis=0, keepdims=True)
            g = jnp.where(g >= kth, -jnp.inf, g)
        sel_sc[h][...] = jnp.where(past & (gate >= kth), 1.0, 0.0)
        m_sc[h][...] = jnp.full_like(m_sc[h], NEG_BIG)
        l_sc[h][...] = jnp.zeros_like(l_sc[h])
        acc_sc[h][...] = jnp.zeros_like(acc_sc[h])

    def update(jb, diag):
        start = pl.multiple_of(jb * t, t)
        pair_feats = [slice(p * pw, (p + 1) * pw) for p in range(heads // 2)]
        scores = [_dot(k_ref[0, pl.ds(start, t), pair_feats[h // 2]], qm[h]) for h in range(heads)]
        probs, alphas = [], []
        for h in range(heads):
            keep = (key <= qry) if diag else (sel_sc[h][pl.ds(jb, 1), :] > 0.5)
            s = jnp.where(keep, scores[h], NEG_BIG)
            m_old = m_sc[h][...]
            m_new = jnp.maximum(m_old, jnp.max(s, axis=0, keepdims=True))
            alpha = jnp.exp2(m_old - m_new)
            p = jnp.exp2(s - m_new)
            l_sc[h][...] = alpha * l_sc[h][...] + jnp.sum(p, axis=0, keepdims=True)
            m_sc[h][...] = m_new
            probs.append(p.astype(BF16))
            alphas.append(alpha)
        for h in range(heads):
            acc_sc[h][...] = alphas[h] * acc_sc[h][...] + _dot(vt_ref[jb, pair_feats[h // 2], :], probs[h])

    def past_block(jb, carry):
        update(jb, False)
        return carry

    lax.fori_loop(0, i, past_block, 0)
    update(i, True)
    top = lax.broadcasted_iota(jnp.int32, (pw, t), 0) < HEAD_DIM
    for p in range(heads // 2):
        out = jnp.where(top, acc_sc[2 * p][...] / l_sc[2 * p][...], acc_sc[2 * p + 1][...] / l_sc[2 * p + 1][...])
        o_ref[:, p * pw:(p + 1) * pw] = out.T.astype(o_ref.dtype)


def _moba(qt, k, vt, kmean, bsz, seq):
    width, n = qt.shape
    t = ATTN_TILE
    nb = seq // MOBA_BLOCK
    assert t == MOBA_BLOCK
    gw = HEAD_GROUP * HEAD_DIM
    nq = seq // t
    return pl.pallas_call(
        _moba_kernel,
        grid=(bsz, width // gw, nq),
        in_specs=[pl.BlockSpec((gw, t), lambda b, g, i: (g, b * nq + i)),
                  pl.BlockSpec((1, seq, gw), lambda b, g, i: (b, 0, g)),
                  pl.BlockSpec((nb, gw, t), lambda b, g, i: (b, g, 0)),
                  pl.BlockSpec((nb, gw), lambda b, g, i: (b, g))],
        out_specs=pl.BlockSpec((t, gw), lambda b, g, i: (b * nq + i, g)),
        out_shape=jax.ShapeDtypeStruct((n, width), BF16),
        scratch_shapes=[pltpu.VMEM(shape, F32) for shape in ((nb, t), (1, t), (1, t), (2 * HEAD_DIM, t))
                        for _ in range(HEAD_GROUP)],
        compiler_params=_cparams("arbitrary", "arbitrary", "arbitrary"),
        name="moba_attention",
    )(qt, k.reshape(bsz, seq, width), vt, kmean)


def _sb_kernel(qt_ref, k_ref, vt_ref, kabs_ref, tri_ref, o_ref, *scratch):
    i = pl.program_id(2)
    t = ATTN_TILE
    pw = 2 * HEAD_DIM
    heads = qt_ref.shape[0] // HEAD_DIM
    run_sc, acc_sc = scratch[:heads], scratch[heads:]
    tri = tri_ref[...]
    key = lax.broadcasted_iota(jnp.int32, (t, t), 0)
    qry = lax.broadcasted_iota(jnp.int32, (t, t), 1)
    strictly_past = key < qry
    kabs = kabs_ref[0]

    qm, zmax = [], []
    for h in range(heads):
        feats = slice((h // 2) * pw, (h // 2 + 1) * pw)
        qh = _pair_rows(qt_ref[feats, :].astype(F32), h)
        qm.append(qh.astype(BF16))
        kmax = jnp.max(kabs[:, h * HEAD_DIM:(h + 1) * HEAD_DIM], axis=1, keepdims=True)
        zmax.append(jnp.sum(jnp.abs(qh), axis=0, keepdims=True) * kmax)
        run_sc[h][...] = jnp.zeros_like(run_sc[h])
        acc_sc[h][...] = jnp.zeros_like(acc_sc[h])

    def block(jb, diag):
        start = pl.multiple_of(jb * t, t)
        pair_feats = [slice(p * pw, (p + 1) * pw) for p in range(heads // 2)]
        zs = [_dot(k_ref[0, pl.ds(start, t), pair_feats[h // 2]], qm[h]) for h in range(heads)]
        loms = []
        for h in range(heads):
            lom = -(jnp.maximum(zs[h], 0.0) + jnp.log(1.0 + jnp.exp(-jnp.abs(zs[h]))))
            if diag:
                lom = jnp.where(strictly_past, lom, 0.0)
            loms.append(lom.astype(BF16))
        suffixes = [_dot(tri, loms[h]) for h in range(heads)]
        weights, worst = [], None
        for h in range(heads):
            run = run_sc[h][...]
            a = jnp.exp(zs[h] + suffixes[h] + run)
            if diag:
                a = jnp.where(strictly_past, a, 0.0)
            weights.append(a.astype(BF16))
            run = run + suffixes[h][0:1, :]
            run_sc[h][...] = run
            reach = run + zmax[h]
            worst = reach if worst is None else jnp.maximum(worst, reach)
        for h in range(heads):
            acc_sc[h][...] += _dot(vt_ref[jb, pair_feats[h // 2], :], weights[h])
        return jnp.max(worst)

    def more(c):
        n, worst = c
        return jnp.logical_and(n < i, worst > SB_LOG_FLOOR)

    lax.while_loop(more, lambda c: (c[0] + 1, block(i - 1 - c[0], False)), (jnp.int32(0), block(i, True)))
    top = lax.broadcasted_iota(jnp.int32, (pw, t), 0) < HEAD_DIM
    for p in range(heads // 2):
        out = jnp.where(top, acc_sc[2 * p][...], acc_sc[2 * p + 1][...])
        o_ref[:, p * pw:(p + 1) * pw] = out.T.astype(o_ref.dtype)


def _stick_breaking(qt, k, vt, kabs, bsz, seq):
    width, n = qt.shape
    t = ATTN_TILE
    gw = SB_HEAD_GROUP * HEAD_DIM
    nq = seq // t
    idx = jnp.arange(t)
    tri = (idx[None, :] >= idx[:, None]).astype(BF16)
    return pl.pallas_call(
        _sb_kernel,
        grid=(bsz, width // gw, nq),
        in_specs=[pl.BlockSpec((gw, t), lambda b, g, i: (g, b * nq + i)),
                  pl.BlockSpec((1, seq, gw), lambda b, g, i: (b, 0, g)),
                  pl.BlockSpec((nq, gw, t), lambda b, g, i: (b, g, 0)),
                  pl.BlockSpec((1, 1, gw), lambda b, g, i: (b, 0, g)),
                  pl.BlockSpec((t, t), lambda b, g, i: (0, 0))],
        out_specs=pl.BlockSpec((t, gw), lambda b, g, i: (b * nq + i, g)),
        out_shape=jax.ShapeDtypeStruct((n, width), BF16),
        scratch_shapes=[pltpu.VMEM(shape, F32) for shape in ((1, t), (2 * HEAD_DIM, t)) for _ in range(SB_HEAD_GROUP)],
        compiler_params=_cparams("arbitrary", "arbitrary", "arbitrary"),
        name="stick_breaking_attention",
    )(qt, k.reshape(bsz, seq, width), vt, kabs, tri)


def _in1_kernel(x_ref, sc_ref, sh_ref, gn_ref, wk_ref, wqt_ref, wvt_ref, qt_ref, k_ref, vt_ref, kabs_ref,
                *, tiles_per_seq):
    i = pl.program_id(0)
    tm = x_ref.shape[0]
    h = _adaln(x_ref[...], gn_ref[...], sc_ref[0], sh_ref[0]).astype(BF16)
    k = _dot(h, wk_ref[...]).astype(BF16)
    k_ref[...] = k
    qt = lax.dot_general(wqt_ref[...], h, NT_DIMS, preferred_element_type=F32)
    qt_ref[...] = (qt * (HEAD_DIM ** -0.5)).astype(BF16)
    vt = lax.dot_general(wvt_ref[...], h, NT_DIMS, preferred_element_type=F32).astype(BF16)
    for blk in range(tm // ATTN_TILE):
        vt_ref[blk] = vt[:, blk * ATTN_TILE:(blk + 1) * ATTN_TILE]
    kabs = jnp.max(jnp.abs(k.astype(F32)), axis=0, keepdims=True)

    @pl.when(i % tiles_per_seq == 0)
    def _first():
        kabs_ref[0] = kabs

    @pl.when(i % tiles_per_seq != 0)
    def _rest():
        kabs_ref[0] = jnp.maximum(kabs_ref[0], kabs)


def _in_proj1(x2d, seq, sc, sh, gain, w_in):
    n, d = x2d.shape
    tm = TOKEN_TILE
    tiles_per_seq = seq // tm
    width = w_in.shape[1] // 3
    blocks = tm // ATTN_TILE
    w_bf = w_in.astype(BF16)
    row_spec = lambda w: pl.BlockSpec((tm, w), lambda i: (i, 0))
    mod_spec = pl.BlockSpec((1, 1, d), lambda i: (i // tiles_per_seq, 0, 0))
    w_spec = pl.BlockSpec((width, d), lambda i: (0, 0))
    return pl.pallas_call(
        functools.partial(_in1_kernel, tiles_per_seq=tiles_per_seq),
        grid=(n // tm,),
        in_specs=[row_spec(d), mod_spec, mod_spec, pl.BlockSpec((1, d), lambda i: (0, 0)),
                  pl.BlockSpec((d, width), lambda i: (0, 0)), w_spec, w_spec],
        out_specs=[pl.BlockSpec((width, tm), lambda i: (0, i)), row_spec(width),
                   pl.BlockSpec((blocks, width, ATTN_TILE), lambda i: (i, 0, 0)),
                   pl.BlockSpec((1, 1, width), lambda i: (i // tiles_per_seq, 0, 0))],
        out_shape=[jax.ShapeDtypeStruct((width, n), BF16), jax.ShapeDtypeStruct((n, width), BF16),
                   jax.ShapeDtypeStruct((n // ATTN_TILE, width, ATTN_TILE), BF16),
                   jax.ShapeDtypeStruct((n // seq, 1, width), F32)],
        compiler_params=_cparams("arbitrary"),
        name="layer1_in_proj",
    )(x2d, sc, sh, gain.reshape(1, d), w_bf[:, width:2 * width], w_bf[:, :width].T, w_bf[:, 2 * width:].T)


def _out_kernel(*refs, widths):
    n_in = len(widths)
    y_refs = refs[:n_in]
    w_ref, x_ref, g1_ref, gn_ref, sc_ref, sh_ref, x1_ref, h2t_ref = refs[n_in:]
    y = None
    off = 0
    for y_ref, wd in zip(y_refs, widths):
        part = _dot(y_ref[...], w_ref[off:off + wd, :])
        y = part if y is None else y + part
        off += wd
    x1 = x_ref[...] + g1_ref[0] * y
    x1_ref[...] = x1
    h2 = _adaln(x1, gn_ref[...], sc_ref[0], sh_ref[0])
    h2t_ref[...] = h2.T.astype(BF16)


def _out_proj(ys, w_out, x2d, seq, g1, gain, sc, sh):
    n, d = x2d.shape
    tm = TOKEN_TILE
    tiles_per_seq = seq // tm
    widths = tuple(y.shape[1] for y in ys)
    row_spec = lambda w: pl.BlockSpec((tm, w), lambda i: (i, 0))
    mod_spec = pl.BlockSpec((1, 1, d), lambda i: (i // tiles_per_seq, 0, 0))
    return pl.pallas_call(
        functools.partial(_out_kernel, widths=widths),
        grid=(n // tm,),
        in_specs=[row_spec(w) for w in widths]
        + [pl.BlockSpec(w_out.shape, lambda i: (0, 0)), row_spec(d), mod_spec,
           pl.BlockSpec((1, d), lambda i: (0, 0)), mod_spec, mod_spec],
        out_specs=[row_spec(d), pl.BlockSpec((d, tm), lambda i: (0, i))],
        out_shape=[jax.ShapeDtypeStruct((n, d), F32), jax.ShapeDtypeStruct((d, n), BF16)],
        compiler_params=_cparams("arbitrary"),
        name="out_proj_residual",
    )(*ys, w_out.astype(BF16), x2d, g1, gain.reshape(1, d), sc, sh)


def _top_values(arrays, n):
    arrays = list(arrays)
    vals = [[] for _ in arrays]
    for _ in range(n):
        for idx, s in enumerate(arrays):
            m = jnp.max(s, axis=0, keepdims=True)
            vals[idx].append(m)
            arrays[idx] = jnp.where(s >= m, -jnp.inf, s)
    return vals


def _oddeven_merge(lo, hi, r):
    step = r * 2
    if step < hi - lo:
        yield from _oddeven_merge(lo, hi, step)
        yield from _oddeven_merge(lo + r, hi, step)
        for i in range(lo + r, hi - r, step):
            yield (i, i + r)
    else:
        yield (lo, lo + r)


def _oddeven_merge_sort(lo, hi):
    if hi - lo >= 1:
        mid = lo + (hi - lo) // 2
        yield from _oddeven_merge_sort(lo, mid)
        yield from _oddeven_merge_sort(mid + 1, hi)
        yield from _oddeven_merge(lo, hi, 1)


F32_SUBLANES = 8


def _top_sorted(s):
    n = s.shape[0] // F32_SUBLANES
    x = [s[F32_SUBLANES * i:F32_SUBLANES * (i + 1), :] for i in range(n)]

    def exchange(i, j):
        x[i], x[j] = jnp.maximum(x[i], x[j]), jnp.minimum(x[i], x[j])

    for i, j in _oddeven_merge_sort(0, n - 1):
        exchange(i, j)
    shift = F32_SUBLANES // 2
    while shift >= 1:
        y = [pltpu.roll(v, shift, 0) for v in x]
        x = [jnp.maximum(x[i], y[n - 1 - i]) for i in range(n)]
        d = n // 2
        while d >= 1:
            for i in range(n):
                if i & d == 0:
                    exchange(i, i + d)
            d //= 2
        shift //= 2
    return [v[0:1, :] for v in x]


def _peer_kernel(h2t_ref, wqt_ref, sk_ref, u_ref, vt_ref, x1_ref, g2_ref, o_ref,
                 rank_sc, p2_sc, cnt_sc, c1_sc, acc_sc):
    j = pl.program_id(1)
    n_steps = pl.num_programs(1)
    nk = PEER_NKEYS
    kk = PEER_TOPK

    @pl.when(j == 0)
    def _route():
        qt = _dot(wqt_ref[...], h2t_ref[...])
        scores = [_dot(sk_ref[h, p], qt[(2 * h + p) * nk:(2 * h + p + 1) * nk, :], precision=HIGHEST)
                  for h in range(PEER_HEADS) for p in range(2)]
        tops = [_top_sorted(sc) for sc in scores]
        assert len(tops[0]) == kk
        cands = [jnp.concatenate([tops[2 * h][r1] + tops[2 * h + 1][r2] for r1 in range(kk) for r2 in range(kk)
                                  if (r1 + 1) * (r2 + 1) <= kk], axis=0) for h in range(PEER_HEADS)]
        bests = _top_values(cands, kk)
        for h in range(PEER_HEADS):
            s1, s2 = scores[2 * h], scores[2 * h + 1]
            a1, a2 = tops[2 * h], tops[2 * h + 1]
            tau = bests[h][kk - 1]
            z = jnp.sum(jnp.where(cands[h] >= tau, jnp.exp(cands[h] - bests[h][0]), 0.0), axis=0, keepdims=True)
            top1 = jnp.concatenate(a1, axis=0)
            per_rank = jnp.zeros_like(top1)
            for r2 in range(kk):
                per_rank = per_rank + jnp.where(top1 + a2[r2] >= tau, 1.0, 0.0)
            cnt = jnp.zeros_like(s1)
            rank2 = jnp.full(s2.shape, float(kk), F32)
            for r in reversed(range(kk)):
                cnt = jnp.where(s1 >= a1[r], per_rank[r:r + 1, :], cnt)
                rank2 = jnp.where(s2 >= a2[r], float(r), rank2)
            e1 = jnp.where(s1 >= a1[kk - 1], jnp.exp(s1 - a1[0]), 0.0)
            e2 = jnp.where(s2 >= a2[kk - 1], jnp.exp(s2 - a2[0]), 0.0)
            rank_sc[h] = rank2.astype(BF16)
            p2_sc[h] = e2.astype(BF16)
            cnt_sc[h] = cnt
            c1_sc[h] = e1 / z
        acc_sc[...] = jnp.zeros_like(acc_sc)

    te = u_ref.shape[0]
    sub = PEER_EXPERT_SUBTILE
    tm = h2t_ref.shape[1]
    pack = 16
    gates = []
    for ai in range(te // nk):
        a = j * (te // nk) + ai
        w = None
        for h in range(PEER_HEADS):
            cnt = jnp.broadcast_to(cnt_sc[h, pl.ds(a, 1), :], (pack, tm)).astype(BF16)
            c1 = jnp.broadcast_to(c1_sc[h, pl.ds(a, 1), :], (pack, tm)).astype(BF16)
            rank2 = rank_sc[h].reshape(nk // pack, pack, tm)
            p2 = p2_sc[h].reshape(nk // pack, pack, tm)
            term = jnp.where(rank2 < cnt[None], p2 * c1[None], jnp.zeros((), BF16))
            w = term if w is None else w + term
        gates.append(w.reshape(nk, tm))
    h2t = h2t_ref[...]
    pre = [_dot(u_ref[si * sub:(si + 1) * sub, :], h2t) for si in range(te // sub)]
    per_sub = sub // nk
    b = jnp.concatenate(
        [jax.nn.gelu(pre[si].astype(BF16)) * jnp.concatenate(gates[si * per_sub:(si + 1) * per_sub], axis=0)
         for si in range(te // sub)], axis=0)
    acc_sc[...] += _dot(vt_ref[...], b)

    @pl.when(j == n_steps - 1)
    def _finish():
        o_ref[...] = x1_ref[...] + g2_ref[0] * acc_sc[...].T


def _peer(h2t, x1, seq, g2, layer, wq_t, subkeys, u_tab, vt_tab):
    d, n = h2t.shape
    tm = PEER_TOKEN_TILE
    te = PEER_EXPERT_TILE
    tiles_per_seq = seq // tm
    ne = u_tab.shape[1]
    heads = subkeys.shape[1]
    per_key = (heads, PEER_NKEYS, tm)
    scratch = [pltpu.VMEM(per_key, BF16), pltpu.VMEM(per_key, BF16), pltpu.VMEM(per_key, F32),
               pltpu.VMEM(per_key, F32), pltpu.VMEM((d, tm), F32)]
    return pl.pallas_call(
        _peer_kernel,
        grid=(n // tm, ne // te),
        in_specs=[
            pl.BlockSpec((d, tm), lambda i, j: (0, i)),
            pl.BlockSpec((None,) + wq_t.shape[1:], lambda i, j: (layer, 0, 0)),
            pl.BlockSpec((None,) + subkeys.shape[1:], lambda i, j: (layer, 0, 0, 0, 0)),
            pl.BlockSpec((None, te, d), lambda i, j: (layer, j, 0)),
            pl.BlockSpec((None, d, te), lambda i, j: (layer, 0, j)),
            pl.BlockSpec((tm, d), lambda i, j: (i, 0)),
            pl.BlockSpec((1, 1, d), lambda i, j: (i // tiles_per_seq, 0, 0)),
        ],
        out_specs=pl.BlockSpec((tm, d), lambda i, j: (i, 0)),
        out_shape=jax.ShapeDtypeStruct((n, d), F32),
        scratch_shapes=scratch,
        compiler_params=_cparams("arbitrary", "arbitrary"),
        name="peer_ffn",
    )(h2t, wq_t, subkeys, u_tab, vt_tab, x1, g2)


def kernel(x, c, w_mod, b_mod, norm_mix, norm_ffn, even_w_in, gmlp_v_gain, gmlp_w_s, gmlp_b_s,
           moba_q_gain, moba_k_gain, even_w_out, odd_w_in, odd_w_out, peer_w_q, peer_subkeys,
           peer_u, peer_v):
    bsz, seq, d = x.shape
    depth = w_mod.shape[0]
    assert seq % TOKEN_TILE == 0 and seq % ATTN_TILE == 0 and seq % PEER_TOKEN_TILE == 0
    mod = _modulation(c, w_mod, b_mod)
    xs = x.reshape(bsz * seq, d)
    wq_t = peer_w_q.transpose(0, 2, 1).astype(BF16)
    u_bf = peer_u.astype(BF16)
    vt_bf = peer_v.transpose(0, 2, 1).astype(BF16)
    for layer in range(depth):
        i = layer // 2
        sh1, sc1, g1, sh2, sc2, g2 = [mod[layer, :, k * d:(k + 1) * d].reshape(bsz, 1, d) for k in range(6)]
        if layer % 2 == 0:
            ya, qt, k, vt, km = _in_proj0(xs, seq, sc1, sh1, norm_mix[layer], even_w_in[i],
                                          gmlp_v_gain[i], gmlp_w_s[i], gmlp_b_s[i],
                                          moba_q_gain[i], moba_k_gain[i])
            yb = _moba(qt, k, vt, km, bsz, seq)
            ys, w_out = [ya, yb], even_w_out[i]
        else:
            qt, k, vt, kabs = _in_proj1(xs, seq, sc1, sh1, norm_mix[layer], odd_w_in[i])
            y = _stick_breaking(qt, k, vt, kabs, bsz, seq)
            ys, w_out = [y], odd_w_out[i]
        x1, h2t = _out_proj(ys, w_out, xs, seq, g1, norm_ffn[layer], sc2, sh2)
        xs = _peer(h2t, x1, seq, g2, layer, wq_t, peer_subkeys, u_bf, vt_bf)
    return xs.reshape(bsz, seq, d)
```

```python
import functools

import jax
import jax.numpy as jnp
from jax import lax
from jax.experimental import pallas as pl
from jax.experimental.pallas import tpu as pltpu

F32 = jnp.float32
BF16 = jnp.bfloat16
HIGHEST = lax.Precision.HIGHEST

EPS = 1e-6
HEAD_DIM = 64
GMLP_GROUPS = 4
GMLP_CHUNK = 128
MOBA_BLOCK = 256
MOBA_TOPK = 3
ROPE_THETA = 500000.0
ROPE_DIM = HEAD_DIM // 4
PEER_HEADS = 8
PEER_NKEYS = 128
PEER_TOPK = 16
NEG_BIG = -1e30
LOG2E = 1.4426950408889634

VMEM_LIMIT_BYTES = 56 * 1024 * 1024
TOKEN_TILE = 512
ATTN_TILE = 256
HEAD_GROUP = 8
SB_LOG_FLOOR = -110.0
PEER_TOKEN_TILE = 512
PEER_EXPERT_TILE = 2048
PEER_EXPERT_SUBTILE = 256
NT_DIMS = (((1,), (1,)), ((), ()))


def _cparams(*sem):
    return pltpu.CompilerParams(dimension_semantics=sem, vmem_limit_bytes=VMEM_LIMIT_BYTES)


def _dot(a, b, **kw):
    return jnp.dot(a, b, preferred_element_type=F32, **kw)


def _group_sum(x, blockdiag):
    hi = x.astype(BF16)
    lo = (x - hi.astype(F32)).astype(BF16)
    return _dot(hi, blockdiag) + _dot(lo, blockdiag)


def _adaln(x, gain, sc, sh):
    ms = jnp.mean(x * x, axis=-1, keepdims=True)
    return x * lax.rsqrt(ms + EPS) * gain * (1.0 + sc) + sh


def _mod_kernel(c_ref, w_ref, b_ref, o_ref):
    c = c_ref[...]
    ca = c * jax.nn.sigmoid(c)
    o_ref[0] = _dot(ca, w_ref[0], precision=HIGHEST) + b_ref[0]


def _modulation(c, w_mod, b_mod):
    depth, d, d6 = w_mod.shape
    bsz = c.shape[0]
    rows = 8
    c_pad = jnp.zeros((rows, d), F32).at[:bsz].set(c)
    tn = 1536
    out = pl.pallas_call(
        _mod_kernel,
        grid=(depth, d6 // tn),
        in_specs=[
            pl.BlockSpec((rows, d), lambda l, n: (0, 0)),
            pl.BlockSpec((1, d, tn), lambda l, n: (l, 0, n)),
            pl.BlockSpec((1, 1, tn), lambda l, n: (l, 0, n)),
        ],
        out_specs=pl.BlockSpec((1, rows, tn), lambda l, n: (l, 0, n)),
        out_shape=jax.ShapeDtypeStruct((depth, rows, d6), F32),
        compiler_params=_cparams("arbitrary", "arbitrary"),
        name="adaln_modulation",
    )(c_pad, w_mod, b_mod.reshape(depth, 1, d6))
    return out[:, :bsz]


def _in0_kernel(x_ref, sc_ref, sh_ref, gn_ref, w_ref, wvt_ref, vgain_ref, ws_ref, bs_ref, qg_ref, kg_ref,
                cos_ref, sina_ref, sinb_ref, bd128_ref, bd64_ref,
                ya_ref, qt_ref, k_ref, vt_ref, km_ref):
    tm = x_ref.shape[0]
    gw = GMLP_GROUPS * GMLP_CHUNK
    h = _adaln(x_ref[...], gn_ref[...], sc_ref[0], sh_ref[0]).astype(BF16)
    proj = _dot(h, w_ref[...])
    ua, va = proj[:, :gw], proj[:, gw:2 * gw]
    qb, kb = proj[:, 2 * gw:3 * gw], proj[:, 3 * gw:4 * gw]
    vt = lax.dot_general(wvt_ref[...], h, NT_DIMS, preferred_element_type=F32).astype(BF16)
    for blk in range(tm // MOBA_BLOCK):
        vt_ref[blk] = vt[:, blk * MOBA_BLOCK:(blk + 1) * MOBA_BLOCK]

    u = jax.nn.gelu(ua)
    gv = jax.nn.gelu(va)
    ss = _group_sum(gv * gv, bd128_ref[...]) * (1.0 / GMLP_CHUNK)
    vn = (gv * lax.rsqrt(ss + EPS) * vgain_ref[...]).astype(BF16)
    r = lax.broadcasted_iota(jnp.int32, (GMLP_CHUNK, GMLP_CHUNK), 0)
    c = lax.broadcasted_iota(jnp.int32, (GMLP_CHUNK, GMLP_CHUNK), 1)
    for g in range(GMLP_GROUPS):
        wg = jnp.where(c <= r, ws_ref[g], 0.0).astype(BF16)
        lanes = slice(g * GMLP_CHUNK, (g + 1) * GMLP_CHUNK)
        for ch in range(tm // GMLP_CHUNK):
            rows = slice(ch * GMLP_CHUNK, (ch + 1) * GMLP_CHUNK)
            mixed = _dot(wg, vn[rows, lanes]) + bs_ref[g]
            ya_ref[rows, lanes] = (u[rows, lanes] * mixed).astype(BF16)

    reps = gw // cos_ref.shape[1]
    cos = jnp.concatenate([cos_ref[...]] * reps, axis=1)
    sina = jnp.concatenate([sina_ref[...]] * reps, axis=1)
    sinb = jnp.concatenate([sinb_ref[...]] * reps, axis=1)
    half = ROPE_DIM // 2

    def norm_rope(t, gain):
        ms = _group_sum(t * t, bd64_ref[...]) * (1.0 / HEAD_DIM)
        tn = t * lax.rsqrt(ms + EPS) * gain
        return tn * cos + pltpu.roll(tn, gw - half, 1) * sina + pltpu.roll(tn, half, 1) * sinb

    q = norm_rope(qb, qg_ref[...]) * (HEAD_DIM ** -0.5 * LOG2E)
    k = norm_rope(kb, kg_ref[...])
    qt_ref[...] = q.T
    k_ref[...] = k.astype(BF16)
    km_ref[0] = jnp.mean(k.reshape(tm // MOBA_BLOCK, MOBA_BLOCK, gw), axis=1)


def _rope_tables(seq):
    half = ROPE_DIM // 2
    inv_freq = ROPE_THETA ** (-jnp.arange(half, dtype=F32) / half)
    ang = jnp.arange(seq, dtype=F32)[:, None] * inv_freq[None, :]
    lane = jnp.arange(2 * HEAD_DIM) % HEAD_DIM
    cos_l = jnp.cos(ang)[:, lane % half]
    sin_l = jnp.sin(ang)[:, lane % half]
    cos = jnp.where(lane < ROPE_DIM, cos_l, 1.0)
    sina = jnp.where(lane < half, -sin_l, 0.0)
    sinb = jnp.where((lane >= half) & (lane < ROPE_DIM), sin_l, 0.0)
    return cos, sina, sinb


def _blockdiag(n, group):
    idx = jnp.arange(n) // group
    return (idx[:, None] == idx[None, :]).astype(BF16)


def _in_proj0(x2d, seq, sc, sh, gain, w_in, v_gain, w_s, b_s, q_gain, k_gain):
    n, d = x2d.shape
    tm = TOKEN_TILE
    tiles_per_seq = seq // tm
    gw = GMLP_GROUPS * GMLP_CHUNK
    n_in = w_in.shape[1]
    cos, sina, sinb = _rope_tables(seq)
    heads = gw // HEAD_DIM
    row_spec = lambda w: pl.BlockSpec((tm, w), lambda i: (i, 0))
    const2 = lambda a: pl.BlockSpec(a.shape, lambda i: (0,) * a.ndim)
    mod_spec = pl.BlockSpec((1, 1, d), lambda i: (i // tiles_per_seq, 0, 0))
    rope_spec = pl.BlockSpec((tm, 2 * HEAD_DIM), lambda i: (i % tiles_per_seq, 0))
    assert n_in == 5 * gw
    w_bf = w_in.astype(BF16)
    args = [x2d, sc, sh, gain.reshape(1, d), w_bf[:, :4 * gw], w_bf[:, 4 * gw:].T, v_gain.reshape(1, gw), w_s,
            jnp.broadcast_to(b_s[:, :, None], b_s.shape + (GMLP_CHUNK,)),
            jnp.tile(q_gain, heads).reshape(1, gw), jnp.tile(k_gain, heads).reshape(1, gw),
            cos, sina, sinb, _blockdiag(gw, GMLP_CHUNK), _blockdiag(gw, HEAD_DIM)]
    in_specs = [row_spec(d), mod_spec, mod_spec] + [const2(a) for a in args[3:11]] \
        + [rope_spec] * 3 + [const2(a) for a in args[14:]]
    blocks = tm // MOBA_BLOCK
    ya, qt, k, vt, km = pl.pallas_call(
        _in0_kernel,
        grid=(n // tm,),
        in_specs=in_specs,
        out_specs=[row_spec(gw), pl.BlockSpec((gw, tm), lambda i: (0, i)), row_spec(gw),
                   pl.BlockSpec((blocks, gw, MOBA_BLOCK), lambda i: (i, 0, 0)),
                   pl.BlockSpec((1, blocks, gw), lambda i: (i, 0, 0))],
        out_shape=[jax.ShapeDtypeStruct((n, gw), BF16), jax.ShapeDtypeStruct((gw, n), F32),
                   jax.ShapeDtypeStruct((n, gw), BF16), jax.ShapeDtypeStruct((n // MOBA_BLOCK, gw, MOBA_BLOCK), BF16),
                   jax.ShapeDtypeStruct((n // tm, blocks, gw), F32)],
        compiler_params=_cparams("arbitrary"),
        name="layer0_in_proj",
    )(*args)
    return ya, qt, k, vt, km.reshape(n // MOBA_BLOCK, gw)


def _pair_rows(ref_rows, h):
    rows = lax.broadcasted_iota(jnp.int32, ref_rows.shape, 0)
    mine = rows < HEAD_DIM if h % 2 == 0 else rows >= HEAD_DIM
    return jnp.where(mine, ref_rows, 0.0)


def _moba_kernel(qt_ref, k_ref, vt_ref, km_ref, o_ref, *scratch):
    i = pl.program_id(2)
    t = ATTN_TILE
    pw = 2 * HEAD_DIM
    heads = qt_ref.shape[0] // HEAD_DIM
    sel_sc, m_sc, l_sc, acc_sc = (scratch[n * heads:(n + 1) * heads] for n in range(4))
    nb = km_ref.shape[0]
    key = lax.broadcasted_iota(jnp.int32, (t, t), 0)
    qry = lax.broadcasted_iota(jnp.int32, (t, t), 1)
    blk = lax.broadcasted_iota(jnp.int32, (nb, t), 0)
    past = blk < i

    qm = []
    for h in range(heads):
        feats = slice((h // 2) * pw, (h // 2 + 1) * pw)
        qh = _pair_rows(qt_ref[feats, :], h)
        qm.append(qh.astype(BF16))
        gate = _dot(km_ref[:, feats], qh, precision=HIGHEST)
        g = jnp.where(past, gate, -jnp.inf)
        kth = None
        for _ in range(MOBA_TOPK):
            kth = jnp.max(g, axis=0, keepdims=True)
            g = jnp.where(g >= kth, -jnp.inf, g)
        sel_sc[h][...] = jnp.where(past & (gate >= kth), 1.0, 0.0)
        m_sc[h][...] = jnp.full_like(m_sc[h], NEG_BIG)
        l_sc[h][...] = jnp.zeros_like(l_sc[h])
        acc_sc[h][...] = jnp.zeros_like(acc_sc[h])

    def update(blocks, diag):
        pair_feats = [slice(p * pw, (p + 1) * pw) for p in range(heads // 2)]
        starts = [pl.multiple_of(jb * t, t) for jb in blocks]
        scores = [[_dot(k_ref[0, pl.ds(st, t), pair_feats[h // 2]], qm[h]) for st in starts] for h in range(heads)]
        probs, alphas = [], []
        for h in range(heads):
            masked = []
            for n, jb in enumerate(blocks):
                keep = (key <= qry) if diag else (sel_sc[h][pl.ds(jb, 1), :] > 0.5)
                masked.append(jnp.where(keep, scores[h][n], NEG_BIG))
            m_old = m_sc[h][...]
            m_new = m_old
            for s in masked:
                m_new = jnp.maximum(m_new, jnp.max(s, axis=0, keepdims=True))
            alpha = jnp.exp2(m_old - m_new)
            ps = [jnp.exp2(s - m_new) for s in masked]
            l_new = alpha * l_sc[h][...]
            for p in ps:
                l_new = l_new + jnp.sum(p, axis=0, keepdims=True)
            l_sc[h][...] = l_new
            m_sc[h][...] = m_new
            probs.append([p.astype(BF16) for p in ps])
            alphas.append(alpha)
        for h in range(heads):
            acc = alphas[h] * acc_sc[h][...]
            for n, jb in enumerate(blocks):
                acc = acc + _dot(vt_ref[jb, pair_feats[h // 2], :], probs[h][n])
            acc_sc[h][...] = acc

    def past_pair(n, carry):
        update([2 * n, 2 * n + 1], False)
        return carry

    lax.fori_loop(0, i // 2, past_pair, 0)

    @pl.when(i % 2 == 1)
    def _odd():
        update([i - 1], False)

    update([i], True)
    top = lax.broadcasted_iota(jnp.int32, (pw, t), 0) < HEAD_DIM
    for p in range(heads // 2):
        out = jnp.where(top, acc_sc[2 * p][...] / l_sc[2 * p][...], acc_sc[2 * p + 1][...] / l_sc[2 * p + 1][...])
        o_ref[:, p * pw:(p + 1) * pw] = out.T.astype(o_ref.dtype)


def _moba(qt, k, vt, kmean, bsz, seq):
    width, n = qt.shape
    t = ATTN_TILE
    nb = seq // MOBA_BLOCK
    assert t == MOBA_BLOCK
    gw = HEAD_GROUP * HEAD_DIM
    nq = seq // t
    return pl.pallas_call(
        _moba_kernel,
        grid=(bsz, width // gw, nq),
        in_specs=[pl.BlockSpec((gw, t), lambda b, g, i: (g, b * nq + i)),
                  pl.BlockSpec((1, seq, gw), lambda b, g, i: (b, 0, g)),
                  pl.BlockSpec((nb, gw, t), lambda b, g, i: (b, g, 0)),
                  pl.BlockSpec((nb, gw), lambda b, g, i: (b, g))],
        out_specs=pl.BlockSpec((t, gw), lambda b, g, i: (b * nq + i, g)),
        out_shape=jax.ShapeDtypeStruct((n, width), BF16),
        scratch_shapes=[pltpu.VMEM(shape, F32) for shape in ((nb, t), (1, t), (1, t), (2 * HEAD_DIM, t))
                        for _ in range(HEAD_GROUP)],
        compiler_params=_cparams("arbitrary", "arbitrary", "arbitrary"),
        name="moba_attention",
    )(qt, k.reshape(bsz, seq, width), vt, kmean)


def _sb_kernel(qt_ref, k_ref, vt_ref, kabs_ref, tri_ref, o_ref, *scratch):
    i = pl.program_id(2)
    t = ATTN_TILE
    pw = 2 * HEAD_DIM
    heads = qt_ref.shape[0] // HEAD_DIM
    run_sc, acc_sc = scratch[:heads], scratch[heads:]
    tri = tri_ref[...]
    key = lax.broadcasted_iota(jnp.int32, (t, t), 0)
    qry = lax.broadcasted_iota(jnp.int32, (t, t), 1)
    strictly_past = key < qry
    kabs = kabs_ref[0]

    qm, zmax = [], []
    for h in range(heads):
        feats = slice((h // 2) * pw, (h // 2 + 1) * pw)
        qh = _pair_rows(qt_ref[feats, :].astype(F32), h)
        qm.append(qh.astype(BF16))
        kmax = jnp.max(kabs[:, h * HEAD_DIM:(h + 1) * HEAD_DIM], axis=1, keepdims=True)
        zmax.append(jnp.sum(jnp.abs(qh), axis=0, keepdims=True) * kmax)
        run_sc[h][...] = jnp.zeros_like(run_sc[h])
        acc_sc[h][...] = jnp.zeros_like(acc_sc[h])

    def block(jb, diag):
        start = pl.multiple_of(jb * t, t)
        pair_feats = [slice(p * pw, (p + 1) * pw) for p in range(heads // 2)]
        zs = [_dot(k_ref[0, pl.ds(start, t), pair_feats[h // 2]], qm[h]) for h in range(heads)]
        loms = []
        for h in range(heads):
            lom = -(jnp.maximum(zs[h], 0.0) + jnp.log(1.0 + jnp.exp(-jnp.abs(zs[h]))))
            if diag:
                lom = jnp.where(strictly_past, lom, 0.0)
            loms.append(lom.astype(BF16))
        suffixes = [_dot(tri, loms[h]) for h in range(heads)]
        weights, worst = [], None
        for h in range(heads):
            run = run_sc[h][...]
            a = jnp.exp(zs[h] + suffixes[h] + run)
            if diag:
                a = jnp.where(strictly_past, a, 0.0)
            weights.append(a.astype(BF16))
            run = run + suffixes[h][0:1, :]
            run_sc[h][...] = run
            reach = run + zmax[h]
            worst = reach if worst is None else jnp.maximum(worst, reach)
        for h in range(heads):
            acc_sc[h][...] += _dot(vt_ref[jb, pair_feats[h // 2], :], weights[h])
        return jnp.max(worst)

    def more(c):
        n, worst = c
        return jnp.logical_and(n < i, worst > SB_LOG_FLOOR)

    lax.while_loop(more, lambda c: (c[0] + 1, block(i - 1 - c[0], False)), (jnp.int32(0), block(i, True)))
    top = lax.broadcasted_iota(jnp.int32, (pw, t), 0) < HEAD_DIM
    for p in range(heads // 2):
        out = jnp.where(top, acc_sc[2 * p][...], acc_sc[2 * p + 1][...])
        o_ref[:, p * pw:(p + 1) * pw] = out.T.astype(o_ref.dtype)


def _stick_breaking(qt, k, vt, kabs, bsz, seq):
    width, n = qt.shape
    t = ATTN_TILE
    gw = HEAD_GROUP * HEAD_DIM
    nq = seq // t
    idx = jnp.arange(t)
    tri = (idx[None, :] >= idx[:, None]).astype(BF16)
    return pl.pallas_call(
        _sb_kernel,
        grid=(bsz, width // gw, nq),
        in_specs=[pl.BlockSpec((gw, t), lambda b, g, i: (g, b * nq + i)),
                  pl.BlockSpec((1, seq, gw), lambda b, g, i: (b, 0, g)),
                  pl.BlockSpec((nq, gw, t), lambda b, g, i: (b, g, 0)),
                  pl.BlockSpec((1, 1, gw), lambda b, g, i: (b, 0, g)),
                  pl.BlockSpec((t, t), lambda b, g, i: (0, 0))],
        out_specs=pl.BlockSpec((t, gw), lambda b, g, i: (b * nq + i, g)),
        out_shape=jax.ShapeDtypeStruct((n, width), BF16),
        scratch_shapes=[pltpu.VMEM(shape, F32) for shape in ((1, t), (2 * HEAD_DIM, t)) for _ in range(HEAD_GROUP)],
        compiler_params=_cparams("arbitrary", "arbitrary", "arbitrary"),
        name="stick_breaking_attention",
    )(qt, k.reshape(bsz, seq, width), vt, kabs, tri)


def _in1_kernel(x_ref, sc_ref, sh_ref, gn_ref, wk_ref, wqt_ref, wvt_ref, qt_ref, k_ref, vt_ref, kabs_ref,
                *, tiles_per_seq):
    i = pl.program_id(0)
    tm = x_ref.shape[0]
    h = _adaln(x_ref[...], gn_ref[...], sc_ref[0], sh_ref[0]).astype(BF16)
    k = _dot(h, wk_ref[...]).astype(BF16)
    k_ref[...] = k
    qt = lax.dot_general(wqt_ref[...], h, NT_DIMS, preferred_element_type=F32)
    qt_ref[...] = (qt * (HEAD_DIM ** -0.5)).astype(BF16)
    vt = lax.dot_general(wvt_ref[...], h, NT_DIMS, preferred_element_type=F32).astype(BF16)
    for blk in range(tm // ATTN_TILE):
        vt_ref[blk] = vt[:, blk * ATTN_TILE:(blk + 1) * ATTN_TILE]
    kabs = jnp.max(jnp.abs(k.astype(F32)), axis=0, keepdims=True)

    @pl.when(i % tiles_per_seq == 0)
    def _first():
        kabs_ref[0] = kabs

    @pl.when(i % tiles_per_seq != 0)
    def _rest():
        kabs_ref[0] = jnp.maximum(kabs_ref[0], kabs)


def _in_proj1(x2d, seq, sc, sh, gain, w_in):
    n, d = x2d.shape
    tm = TOKEN_TILE
    tiles_per_seq = seq // tm
    width = w_in.shape[1] // 3
    blocks = tm // ATTN_TILE
    w_bf = w_in.astype(BF16)
    row_spec = lambda w: pl.BlockSpec((tm, w), lambda i: (i, 0))
    mod_spec = pl.BlockSpec((1, 1, d), lambda i: (i // tiles_per_seq, 0, 0))
    w_spec = pl.BlockSpec((width, d), lambda i: (0, 0))
    return pl.pallas_call(
        functools.partial(_in1_kernel, tiles_per_seq=tiles_per_seq),
        grid=(n // tm,),
        in_specs=[row_spec(d), mod_spec, mod_spec, pl.BlockSpec((1, d), lambda i: (0, 0)),
                  pl.BlockSpec((d, width), lambda i: (0, 0)), w_spec, w_spec],
        out_specs=[pl.BlockSpec((width, tm), lambda i: (0, i)), row_spec(width),
                   pl.BlockSpec((blocks, width, ATTN_TILE), lambda i: (i, 0, 0)),
                   pl.BlockSpec((1, 1, width), lambda i: (i // tiles_per_seq, 0, 0))],
        out_shape=[jax.ShapeDtypeStruct((width, n), BF16), jax.ShapeDtypeStruct((n, width), BF16),
                   jax.ShapeDtypeStruct((n // ATTN_TILE, width, ATTN_TILE), BF16),
                   jax.ShapeDtypeStruct((n // seq, 1, width), F32)],
        compiler_params=_cparams("arbitrary"),
        name="layer1_in_proj",
    )(x2d, sc, sh, gain.reshape(1, d), w_bf[:, width:2 * width], w_bf[:, :width].T, w_bf[:, 2 * width:].T)


def _out_kernel(*refs, widths):
    n_in = len(widths)
    y_refs = refs[:n_in]
    w_ref, x_ref, g1_ref, gn_ref, sc_ref, sh_ref, x1_ref, h2t_ref = refs[n_in:]
    y = None
    off = 0
    for y_ref, wd in zip(y_refs, widths):
        part = _dot(y_ref[...], w_ref[off:off + wd, :])
        y = part if y is None else y + part
        off += wd
    x1 = x_ref[...] + g1_ref[0] * y
    x1_ref[...] = x1
    h2 = _adaln(x1, gn_ref[...], sc_ref[0], sh_ref[0])
    h2t_ref[...] = h2.T.astype(BF16)


def _out_proj(ys, w_out, x2d, seq, g1, gain, sc, sh):
    n, d = x2d.shape
    tm = TOKEN_TILE
    tiles_per_seq = seq // tm
    widths = tuple(y.shape[1] for y in ys)
    row_spec = lambda w: pl.BlockSpec((tm, w), lambda i: (i, 0))
    mod_spec = pl.BlockSpec((1, 1, d), lambda i: (i // tiles_per_seq, 0, 0))
    return pl.pallas_call(
        functools.partial(_out_kernel, widths=widths),
        grid=(n // tm,),
        in_specs=[row_spec(w) for w in widths]
        + [pl.BlockSpec(w_out.shape, lambda i: (0, 0)), row_spec(d), mod_spec,
           pl.BlockSpec((1, d), lambda i: (0, 0)), mod_spec, mod_spec],
        out_specs=[row_spec(d), pl.BlockSpec((d, tm), lambda i: (0, i))],
        out_shape=[jax.ShapeDtypeStruct((n, d), F32), jax.ShapeDtypeStruct((d, n), BF16)],
        compiler_params=_cparams("arbitrary"),
        name="out_proj_residual",
    )(*ys, w_out.astype(BF16), x2d, g1, gain.reshape(1, d), sc, sh)


def _top_values(arrays, n):
    arrays = list(arrays)
    vals = [[] for _ in arrays]
    for _ in range(n):
        for idx, s in enumerate(arrays):
            m = jnp.max(s, axis=0, keepdims=True)
            vals[idx].append(m)
            arrays[idx] = jnp.where(s >= m, -jnp.inf, s)
    return vals


def _oddeven_merge(lo, hi, r):
    step = r * 2
    if step < hi - lo:
        yield from _oddeven_merge(lo, hi, step)
        yield from _oddeven_merge(lo + r, hi, step)
        for i in range(lo + r, hi - r, step):
            yield (i, i + r)
    else:
        yield (lo, lo + r)


def _oddeven_merge_sort(lo, hi):
    if hi - lo >= 1:
        mid = lo + (hi - lo) // 2
        yield from _oddeven_merge_sort(lo, mid)
        yield from _oddeven_merge_sort(mid + 1, hi)
        yield from _oddeven_merge(lo, hi, 1)


F32_SUBLANES = 8


def _top_sorted(s):
    n = s.shape[0] // F32_SUBLANES
    x = [s[F32_SUBLANES * i:F32_SUBLANES * (i + 1), :] for i in range(n)]

    def exchange(i, j):
        x[i], x[j] = jnp.maximum(x[i], x[j]), jnp.minimum(x[i], x[j])

    for i, j in _oddeven_merge_sort(0, n - 1):
        exchange(i, j)
    shift = F32_SUBLANES // 2
    while shift >= 1:
        y = [pltpu.roll(v, shift, 0) for v in x]
        x = [jnp.maximum(x[i], y[n - 1 - i]) for i in range(n)]
        d = n // 2
        while d >= 1:
            for i in range(n):
                if i & d == 0:
                    exchange(i, i + d)
            d //= 2
        shift //= 2
    return [v[0:1, :] for v in x]


def _peer_kernel(h2t_ref, wqt_ref, sk_ref, u_ref, vt_ref, x1_ref, g2_ref, o_ref,
                 rank_sc, p2_sc, cnt_sc, c1_sc, acc_sc):
    j = pl.program_id(1)
    n_steps = pl.num_programs(1)
    nk = PEER_NKEYS
    kk = PEER_TOPK

    @pl.when(j == 0)
    def _route():
        qt = _dot(wqt_ref[...], h2t_ref[...])
        scores = [_dot(sk_ref[h, p], qt[(2 * h + p) * nk:(2 * h + p + 1) * nk, :], precision=HIGHEST)
                  for h in range(PEER_HEADS) for p in range(2)]
        tops = [_top_sorted(sc) for sc in scores]
        assert len(tops[0]) == kk
        cands = [jnp.concatenate([tops[2 * h][r1] + tops[2 * h + 1][r2] for r1 in range(kk) for r2 in range(kk)
                                  if (r1 + 1) * (r2 + 1) <= kk], axis=0) for h in range(PEER_HEADS)]
        bests = _top_values(cands, kk)
        for h in range(PEER_HEADS):
            s1, s2 = scores[2 * h], scores[2 * h + 1]
            a1, a2 = tops[2 * h], tops[2 * h + 1]
            tau = bests[h][kk - 1]
            z = jnp.sum(jnp.where(cands[h] >= tau, jnp.exp(cands[h] - bests[h][0]), 0.0), axis=0, keepdims=True)
            top1 = jnp.concatenate(a1, axis=0)
            per_rank = jnp.zeros_like(top1)
            for r2 in range(kk):
                per_rank = per_rank + jnp.where(top1 + a2[r2] >= tau, 1.0, 0.0)
            cnt = jnp.zeros_like(s1)
            rank2 = jnp.full(s2.shape, float(kk), F32)
            for r in reversed(range(kk)):
                cnt = jnp.where(s1 >= a1[r], per_rank[r:r + 1, :], cnt)
                rank2 = jnp.where(s2 >= a2[r], float(r), rank2)
            e1 = jnp.where(s1 >= a1[kk - 1], jnp.exp(s1 - a1[0]), 0.0)
            e2 = jnp.where(s2 >= a2[kk - 1], jnp.exp(s2 - a2[0]), 0.0)
            rank_sc[h] = rank2.astype(BF16)
            p2_sc[h] = e2.astype(BF16)
            cnt_sc[h] = cnt
            c1_sc[h] = e1 / z
        acc_sc[...] = jnp.zeros_like(acc_sc)

    te = u_ref.shape[0]
    sub = PEER_EXPERT_SUBTILE
    tm = h2t_ref.shape[1]
    pack = 16
    gates = []
    for ai in range(te // nk):
        a = j * (te // nk) + ai
        w = None
        for h in range(PEER_HEADS):
            cnt = jnp.broadcast_to(cnt_sc[h, pl.ds(a, 1), :], (pack, tm)).astype(BF16)
            c1 = jnp.broadcast_to(c1_sc[h, pl.ds(a, 1), :], (pack, tm)).astype(BF16)
            rank2 = rank_sc[h].reshape(nk // pack, pack, tm)
            p2 = p2_sc[h].reshape(nk // pack, pack, tm)
            term = jnp.where(rank2 < cnt[None], p2 * c1[None], jnp.zeros((), BF16))
            w = term if w is None else w + term
        gates.append(w.reshape(nk, tm))
    h2t = h2t_ref[...]
    pre = [_dot(u_ref[si * sub:(si + 1) * sub, :], h2t) for si in range(te // sub)]
    per_sub = sub // nk
    b = jnp.concatenate(
        [jax.nn.gelu(pre[si].astype(BF16)) * jnp.concatenate(gates[si * per_sub:(si + 1) * per_sub], axis=0)
         for si in range(te // sub)], axis=0)
    acc_sc[...] += _dot(vt_ref[...], b)

    @pl.when(j == n_steps - 1)
    def _finish():
        o_ref[...] = x1_ref[...] + g2_ref[0] * acc_sc[...].T


def _peer(h2t, x1, seq, g2, layer, wq_t, subkeys, u_tab, vt_tab):
    d, n = h2t.shape
    tm = PEER_TOKEN_TILE
    te = PEER_EXPERT_TILE
    tiles_per_seq = seq // tm
    ne = u_tab.shape[1]
    heads = subkeys.shape[1]
    per_key = (heads, PEER_NKEYS, tm)
    scratch = [pltpu.VMEM(per_key, BF16), pltpu.VMEM(per_key, BF16), pltpu.VMEM(per_key, F32),
               pltpu.VMEM(per_key, F32), pltpu.VMEM((d, tm), F32)]
    return pl.pallas_call(
        _peer_kernel,
        grid=(n // tm, ne // te),
        in_specs=[
            pl.BlockSpec((d, tm), lambda i, j: (0, i)),
            pl.BlockSpec((None,) + wq_t.shape[1:], lambda i, j: (layer, 0, 0)),
            pl.BlockSpec((None,) + subkeys.shape[1:], lambda i, j: (layer, 0, 0, 0, 0)),
            pl.BlockSpec((None, te, d), lambda i, j: (layer, j, 0)),
            pl.BlockSpec((None, d, te), lambda i, j: (layer, 0, j)),
            pl.BlockSpec((tm, d), lambda i, j: (i, 0)),
            pl.BlockSpec((1, 1, d), lambda i, j: (i // tiles_per_seq, 0, 0)),
        ],
        out_specs=pl.BlockSpec((tm, d), lambda i, j: (i, 0)),
        out_shape=jax.ShapeDtypeStruct((n, d), F32),
        scratch_shapes=scratch,
        compiler_params=_cparams("arbitrary", "arbitrary"),
        name="peer_ffn",
    )(h2t, wq_t, subkeys, u_tab, vt_tab, x1, g2)


def kernel(x, c, w_mod, b_mod, norm_mix, norm_ffn, even_w_in, gmlp_v_gain, gmlp_w_s, gmlp_b_s,
           moba_q_gain, moba_k_gain, even_w_out, odd_w_in, odd_w_out, peer_w_q, peer_subkeys,
           peer_u, peer_v):
    bsz, seq, d = x.shape
    depth = w_mod.shape[0]
    assert seq % TOKEN_TILE == 0 and seq % ATTN_TILE == 0 and seq % PEER_TOKEN_TILE == 0
    mod = _modulation(c, w_mod, b_mod)
    xs = x.reshape(bsz * seq, d)
    wq_t = peer_w_q.transpose(0, 2, 1).astype(BF16)
    u_bf = peer_u.astype(BF16)
    vt_bf = peer_v.transpose(0, 2, 1).astype(BF16)
    for layer in range(depth):
        i = layer // 2
        sh1, sc1, g1, sh2, sc2, g2 = [mod[layer, :, k * d:(k + 1) * d].reshape(bsz, 1, d) for k in range(6)]
        if layer % 2 == 0:
            ya, qt, k, vt, km = _in_proj0(xs, seq, sc1, sh1, norm_mix[layer], even_w_in[i],
                                          gmlp_v_gain[i], gmlp_w_s[i], gmlp_b_s[i],
                                          moba_q_gain[i], moba_k_gain[i])
            yb = _moba(qt, k, vt, km, bsz, seq)
            ys, w_out = [ya, yb], even_w_out[i]
        else:
            qt, k, vt, kabs = _in_proj1(xs, seq, sc1, sh1, norm_mix[layer], odd_w_in[i])
            y = _stick_breaking(qt, k, vt, kabs, bsz, seq)
            ys, w_out = [y], odd_w_out[i]
        x1, h2t = _out_proj(ys, w_out, xs, seq, g1, norm_ffn[layer], sc2, sh2)
        xs = _peer(h2t, x1, seq, g2, layer, wq_t, peer_subkeys, u_bf, vt_bf)
    return xs.reshape(bsz, seq, d)
```

```python
import functools

import jax
import jax.numpy as jnp
from jax import lax
from jax.experimental import pallas as pl
from jax.experimental.pallas import tpu as pltpu

F32 = jnp.float32
BF16 = jnp.bfloat16
HIGHEST = lax.Precision.HIGHEST

EPS = 1e-6
HEAD_DIM = 64
GMLP_GROUPS = 4
GMLP_CHUNK = 128
MOBA_BLOCK = 256
MOBA_TOPK = 3
ROPE_THETA = 500000.0
ROPE_DIM = HEAD_DIM // 4
PEER_HEADS = 8
PEER_NKEYS = 128
PEER_TOPK = 16
NEG_BIG = -1e30
LOG2E = 1.4426950408889634

VMEM_LIMIT_BYTES = 56 * 1024 * 1024
TOKEN_TILE = 512
ATTN_TILE = 256
HEAD_GROUP = 8
SB_LOG_FLOOR = -110.0
PEER_TOKEN_TILE = 512
PEER_EXPERT_TILE = 2048
PEER_EXPERT_SUBTILE = 256
NT_DIMS = (((1,), (1,)), ((), ()))


def _cparams(*sem):
    return pltpu.CompilerParams(dimension_semantics=sem, vmem_limit_bytes=VMEM_LIMIT_BYTES)


def _dot(a, b, **kw):
    return jnp.dot(a, b, preferred_element_type=F32, **kw)


def _group_sum(x, blockdiag):
    hi = x.astype(BF16)
    lo = (x - hi.astype(F32)).astype(BF16)
    return _dot(hi, blockdiag) + _dot(lo, blockdiag)


def _adaln(x, gain, sc, sh):
    ms = jnp.mean(x * x, axis=-1, keepdims=True)
    return x * lax.rsqrt(ms + EPS) * gain * (1.0 + sc) + sh


def _mod_kernel(c_ref, w_ref, b_ref, o_ref):
    c = c_ref[...]
    ca = c * jax.nn.sigmoid(c)
    o_ref[0] = _dot(ca, w_ref[0], precision=HIGHEST) + b_ref[0]


def _modulation(c, w_mod, b_mod):
    depth, d, d6 = w_mod.shape
    bsz = c.shape[0]
    rows = 8
    c_pad = jnp.zeros((rows, d), F32).at[:bsz].set(c)
    tn = 1536
    out = pl.pallas_call(
        _mod_kernel,
        grid=(depth, d6 // tn),
        in_specs=[
            pl.BlockSpec((rows, d), lambda l, n: (0, 0)),
            pl.BlockSpec((1, d, tn), lambda l, n: (l, 0, n)),
            pl.BlockSpec((1, 1, tn), lambda l, n: (l, 0, n)),
        ],
        out_specs=pl.BlockSpec((1, rows, tn), lambda l, n: (l, 0, n)),
        out_shape=jax.ShapeDtypeStruct((depth, rows, d6), F32),
        compiler_params=_cparams("arbitrary", "arbitrary"),
        name="adaln_modulation",
    )(c_pad, w_mod, b_mod.reshape(depth, 1, d6))
    return out[:, :bsz]


def _in0_kernel(x_ref, sc_ref, sh_ref, gn_ref, w_ref, wvt_ref, vgain_ref, ws_ref, bs_ref, qg_ref, kg_ref,
                cos_ref, sina_ref, sinb_ref, bd128_ref, bd64_ref,
                ya_ref, qt_ref, k_ref, vt_ref, km_ref):
    tm = x_ref.shape[0]
    gw = GMLP_GROUPS * GMLP_CHUNK
    h = _adaln(x_ref[...], gn_ref[...], sc_ref[0], sh_ref[0]).astype(BF16)
    proj = _dot(h, w_ref[...])
    ua, va = proj[:, :gw], proj[:, gw:2 * gw]
    qb, kb = proj[:, 2 * gw:3 * gw], proj[:, 3 * gw:4 * gw]
    vt = lax.dot_general(wvt_ref[...], h, NT_DIMS, preferred_element_type=F32).astype(BF16)
    for blk in range(tm // MOBA_BLOCK):
        vt_ref[blk] = vt[:, blk * MOBA_BLOCK:(blk + 1) * MOBA_BLOCK]

    u = jax.nn.gelu(ua)
    gv = jax.nn.gelu(va)
    ss = _group_sum(gv * gv, bd128_ref[...]) * (1.0 / GMLP_CHUNK)
    vn = (gv * lax.rsqrt(ss + EPS) * vgain_ref[...]).astype(BF16)
    r = lax.broadcasted_iota(jnp.int32, (GMLP_CHUNK, GMLP_CHUNK), 0)
    c = lax.broadcasted_iota(jnp.int32, (GMLP_CHUNK, GMLP_CHUNK), 1)
    for g in range(GMLP_GROUPS):
        wg = jnp.where(c <= r, ws_ref[g], 0.0).astype(BF16)
        lanes = slice(g * GMLP_CHUNK, (g + 1) * GMLP_CHUNK)
        for ch in range(tm // GMLP_CHUNK):
            rows = slice(ch * GMLP_CHUNK, (ch + 1) * GMLP_CHUNK)
            mixed = _dot(wg, vn[rows, lanes]) + bs_ref[g]
            ya_ref[rows, lanes] = (u[rows, lanes] * mixed).astype(BF16)

    reps = gw // cos_ref.shape[1]
    cos = jnp.concatenate([cos_ref[...]] * reps, axis=1)
    sina = jnp.concatenate([sina_ref[...]] * reps, axis=1)
    sinb = jnp.concatenate([sinb_ref[...]] * reps, axis=1)
    half = ROPE_DIM // 2

    def norm_rope(t, gain):
        ms = _group_sum(t * t, bd64_ref[...]) * (1.0 / HEAD_DIM)
        tn = t * lax.rsqrt(ms + EPS) * gain
        return tn * cos + pltpu.roll(tn, gw - half, 1) * sina + pltpu.roll(tn, half, 1) * sinb

    q = norm_rope(qb, qg_ref[...]) * (HEAD_DIM ** -0.5 * LOG2E)
    k = norm_rope(kb, kg_ref[...])
    qt_ref[...] = q.T
    k_ref[...] = k.astype(BF16)
    km_ref[0] = jnp.mean(k.reshape(tm // MOBA_BLOCK, MOBA_BLOCK, gw), axis=1)


def _rope_tables(seq):
    half = ROPE_DIM // 2
    inv_freq = ROPE_THETA ** (-jnp.arange(half, dtype=F32) / half)
    ang = jnp.arange(seq, dtype=F32)[:, None] * inv_freq[None, :]
    lane = jnp.arange(2 * HEAD_DIM) % HEAD_DIM
    cos_l = jnp.cos(ang)[:, lane % half]
    sin_l = jnp.sin(ang)[:, lane % half]
    cos = jnp.where(lane < ROPE_DIM, cos_l, 1.0)
    sina = jnp.where(lane < half, -sin_l, 0.0)
    sinb = jnp.where((lane >= half) & (lane < ROPE_DIM), sin_l, 0.0)
    return cos, sina, sinb


def _blockdiag(n, group):
    idx = jnp.arange(n) // group
    return (idx[:, None] == idx[None, :]).astype(BF16)


def _in_proj0(x2d, seq, sc, sh, gain, w_in, v_gain, w_s, b_s, q_gain, k_gain):
    n, d = x2d.shape
    tm = TOKEN_TILE
    tiles_per_seq = seq // tm
    gw = GMLP_GROUPS * GMLP_CHUNK
    n_in = w_in.shape[1]
    cos, sina, sinb = _rope_tables(seq)
    heads = gw // HEAD_DIM
    row_spec = lambda w: pl.BlockSpec((tm, w), lambda i: (i, 0))
    const2 = lambda a: pl.BlockSpec(a.shape, lambda i: (0,) * a.ndim)
    mod_spec = pl.BlockSpec((1, 1, d), lambda i: (i // tiles_per_seq, 0, 0))
    rope_spec = pl.BlockSpec((tm, 2 * HEAD_DIM), lambda i: (i % tiles_per_seq, 0))
    assert n_in == 5 * gw
    w_bf = w_in.astype(BF16)
    args = [x2d, sc, sh, gain.reshape(1, d), w_bf[:, :4 * gw], w_bf[:, 4 * gw:].T, v_gain.reshape(1, gw), w_s,
            jnp.broadcast_to(b_s[:, :, None], b_s.shape + (GMLP_CHUNK,)),
            jnp.tile(q_gain, heads).reshape(1, gw), jnp.tile(k_gain, heads).reshape(1, gw),
            cos, sina, sinb, _blockdiag(gw, GMLP_CHUNK), _blockdiag(gw, HEAD_DIM)]
    in_specs = [row_spec(d), mod_spec, mod_spec] + [const2(a) for a in args[3:11]] \
        + [rope_spec] * 3 + [const2(a) for a in args[14:]]
    blocks = tm // MOBA_BLOCK
    ya, qt, k, vt, km = pl.pallas_call(
        _in0_kernel,
        grid=(n // tm,),
        in_specs=in_specs,
        out_specs=[row_spec(gw), pl.BlockSpec((gw, tm), lambda i: (0, i)), row_spec(gw),
                   pl.BlockSpec((blocks, gw, MOBA_BLOCK), lambda i: (i, 0, 0)),
                   pl.BlockSpec((1, blocks, gw), lambda i: (i, 0, 0))],
        out_shape=[jax.ShapeDtypeStruct((n, gw), BF16), jax.ShapeDtypeStruct((gw, n), F32),
                   jax.ShapeDtypeStruct((n, gw), BF16), jax.ShapeDtypeStruct((n // MOBA_BLOCK, gw, MOBA_BLOCK), BF16),
                   jax.ShapeDtypeStruct((n // tm, blocks, gw), F32)],
        compiler_params=_cparams("arbitrary"),
        name="layer0_in_proj",
    )(*args)
    return ya, qt, k, vt, km.reshape(n // MOBA_BLOCK, gw)


def _pair_rows(ref_rows, h):
    rows = lax.broadcasted_iota(jnp.int32, ref_rows.shape, 0)
    mine = rows < HEAD_DIM if h % 2 == 0 else rows >= HEAD_DIM
    return jnp.where(mine, ref_rows, 0.0)


def _moba_kernel(qt_ref, k_ref, vt_ref, km_ref, o_ref, *scratch):
    i = pl.program_id(2)
    t = ATTN_TILE
    pw = 2 * HEAD_DIM
    heads = qt_ref.shape[0] // HEAD_DIM
    sel_sc, m_sc, l_sc, acc_sc = (scratch[n * heads:(n + 1) * heads] for n in range(4))
    nb = km_ref.shape[0]
    key = lax.broadcasted_iota(jnp.int32, (t, t), 0)
    qry = lax.broadcasted_iota(jnp.int32, (t, t), 1)
    blk = lax.broadcasted_iota(jnp.int32, (nb, t), 0)
    past = blk < i

    qm = []
    for h in range(heads):
        feats = slice((h // 2) * pw, (h // 2 + 1) * pw)
        qh = _pair_rows(qt_ref[feats, :], h)
        qm.append(qh.astype(BF16))
        gate = _dot(km_ref[:, feats], qh, precision=HIGHEST)
        g = jnp.where(past, gate, -jnp.inf)
        kth = None
        for _ in range(MOBA_TOPK):
            kth = jnp.max(g, axis=0, keepdims=True)
            g = jnp.where(g >= kth, -jnp.inf, g)
        sel_sc[h][...] = jnp.where(past & (gate >= kth), 1.0, 0.0)
        m_sc[h][...] = jnp.full_like(m_sc[h], NEG_BIG)
        l_sc[h][...] = jnp.zeros_like(l_sc[h])
        acc_sc[h][...] = jnp.zeros_like(acc_sc[h])

    def update(blocks, diag):
        pair_feats = [slice(p * pw, (p + 1) * pw) for p in range(heads // 2)]
        starts = [pl.multiple_of(jb * t, t) for jb in blocks]
        scores = [[_dot(k_ref[0, pl.ds(st, t), pair_feats[h // 2]], qm[h]) for st in starts] for h in range(heads)]
        probs, alphas = [], []
        for h in range(heads):
            masked = []
            for n, jb in enumerate(blocks):
                keep = (key <= qry) if diag else (sel_sc[h][pl.ds(jb, 1), :] > 0.5)
                masked.append(jnp.where(keep, scores[h][n], NEG_BIG))
            m_old = m_sc[h][...]
            m_new = m_old
            for s in masked:
                m_new = jnp.maximum(m_new, jnp.max(s, axis=0, keepdims=True))
            alpha = jnp.exp2(m_old - m_new)
            ps = [jnp.exp2(s - m_new) for s in masked]
            l_new = alpha * l_sc[h][...]
            for p in ps:
                l_new = l_new + jnp.sum(p, axis=0, keepdims=True)
            l_sc[h][...] = l_new
            m_sc[h][...] = m_new
            probs.append([p.astype(BF16) for p in ps])
            alphas.append(alpha)
        for h in range(heads):
            acc = alphas[h] * acc_sc[h][...]
            for n, jb in enumerate(blocks):
                acc = acc + _dot(vt_ref[jb, pair_feats[h // 2], :], probs[h][n])
            acc_sc[h][...] = acc

    def past_pair(n, carry):
        update([2 * n, 2 * n + 1], False)
        return carry

    lax.fori_loop(0, i // 2, past_pair, 0)

    @pl.when(i % 2 == 1)
    def _odd():
        update([i - 1], False)

    update([i], True)
    top = lax.broadcasted_iota(jnp.int32, (pw, t), 0) < HEAD_DIM
    for p in range(heads // 2):
        out = jnp.where(top, acc_sc[2 * p][...] / l_sc[2 * p][...], acc_sc[2 * p + 1][...] / l_sc[2 * p + 1][...])
        o_ref[:, p * pw:(p + 1) * pw] = out.T.astype(o_ref.dtype)


def _moba(qt, k, vt, kmean, bsz, seq):
    width, n = qt.shape
    t = ATTN_TILE
    nb = seq // MOBA_BLOCK
    assert t == MOBA_BLOCK
    gw = HEAD_GROUP * HEAD_DIM
    nq = seq // t
    return pl.pallas_call(
        _moba_kernel,
        grid=(bsz, width // gw, nq),
        in_specs=[pl.BlockSpec((gw, t), lambda b, g, i: (g, b * nq + i)),
                  pl.BlockSpec((1, seq, gw), lambda b, g, i: (b, 0, g)),
                  pl.BlockSpec((nb, gw, t), lambda b, g, i: (b, g, 0)),
                  pl.BlockSpec((nb, gw), lambda b, g, i: (b, g))],
        out_specs=pl.BlockSpec((t, gw), lambda b, g, i: (b * nq + i, g)),
        out_shape=jax.ShapeDtypeStruct((n, width), BF16),
        scratch_shapes=[pltpu.VMEM(shape, F32) for shape in ((nb, t), (1, t), (1, t), (2 * HEAD_DIM, t))
                        for _ in range(HEAD_GROUP)],
        compiler_params=_cparams("arbitrary", "arbitrary", "arbitrary"),
        name="moba_attention",
    )(qt, k.reshape(bsz, seq, width), vt, kmean)


def _sb_kernel(qt_ref, k_ref, vt_ref, kabs_ref, tri_ref, o_ref, *scratch):
    i = pl.program_id(2)
    t = ATTN_TILE
    pw = 2 * HEAD_DIM
    heads = qt_ref.shape[0] // HEAD_DIM
    run_sc, acc_sc = scratch[:heads], scratch[heads:]
    tri = tri_ref[...]
    key = lax.broadcasted_iota(jnp.int32, (t, t), 0)
    qry = lax.broadcasted_iota(jnp.int32, (t, t), 1)
    strictly_past = key < qry
    kabs = kabs_ref[0]

    qm, zmax = [], []
    for h in range(heads):
        feats = slice((h // 2) * pw, (h // 2 + 1) * pw)
        qh = _pair_rows(qt_ref[feats, :].astype(F32), h)
        qm.append(qh.astype(BF16))
        kmax = jnp.max(kabs[:, h * HEAD_DIM:(h + 1) * HEAD_DIM], axis=1, keepdims=True)
        zmax.append(jnp.sum(jnp.abs(qh), axis=0, keepdims=True) * kmax)
        run_sc[h][...] = jnp.zeros_like(run_sc[h])
        acc_sc[h][...] = jnp.zeros_like(acc_sc[h])

    def visit(specs):
        pair_feats = [slice(p * pw, (p + 1) * pw) for p in range(heads // 2)]
        starts = [pl.multiple_of(jb * t, t) for jb, _ in specs]
        zs = [[_dot(k_ref[0, pl.ds(st, t), pair_feats[h // 2]], qm[h]) for st in starts] for h in range(heads)]
        loms = []
        for h in range(heads):
            per_block = []
            for n, (_, diag) in enumerate(specs):
                lom = -(jnp.maximum(zs[h][n], 0.0) + jnp.log(1.0 + jnp.exp(-jnp.abs(zs[h][n]))))
                if diag:
                    lom = jnp.where(strictly_past, lom, 0.0)
                per_block.append(lom.astype(BF16))
            loms.append(per_block)
        suffixes = [[_dot(tri, lom) for lom in loms[h]] for h in range(heads)]
        weights, worst = [], None
        for h in range(heads):
            run = run_sc[h][...]
            per_block = []
            for n, (_, diag) in enumerate(specs):
                a = jnp.exp(zs[h][n] + suffixes[h][n] + run)
                if diag:
                    a = jnp.where(strictly_past, a, 0.0)
                per_block.append(a.astype(BF16))
                run = run + suffixes[h][n][0:1, :]
            weights.append(per_block)
            run_sc[h][...] = run
            reach = run + zmax[h]
            worst = reach if worst is None else jnp.maximum(worst, reach)
        for h in range(heads):
            acc = acc_sc[h][...]
            for n, (jb, _) in enumerate(specs):
                acc = acc + _dot(vt_ref[jb, pair_feats[h // 2], :], weights[h][n])
            acc_sc[h][...] = acc
        return jnp.max(worst)

    first = lax.cond(i > 0, lambda: visit([(i, True), (i - 1, False)]), lambda: visit([(i, True)]))

    def more(c):
        n, worst = c
        return jnp.logical_and(n < i, worst > SB_LOG_FLOOR)

    lax.while_loop(more, lambda c: (c[0] + 1, visit([(i - 1 - c[0], False)])), (jnp.int32(1), first))
    top = lax.broadcasted_iota(jnp.int32, (pw, t), 0) < HEAD_DIM
    for p in range(heads // 2):
        out = jnp.where(top, acc_sc[2 * p][...], acc_sc[2 * p + 1][...])
        o_ref[:, p * pw:(p + 1) * pw] = out.T.astype(o_ref.dtype)


def _stick_breaking(qt, k, vt, kabs, bsz, seq):
    width, n = qt.shape
    t = ATTN_TILE
    gw = HEAD_GROUP * HEAD_DIM
    nq = seq // t
    idx = jnp.arange(t)
    tri = (idx[None, :] >= idx[:, None]).astype(BF16)
    return pl.pallas_call(
        _sb_kernel,
        grid=(bsz, width // gw, nq),
        in_specs=[pl.BlockSpec((gw, t), lambda b, g, i: (g, b * nq + i)),
                  pl.BlockSpec((1, seq, gw), lambda b, g, i: (b, 0, g)),
                  pl.BlockSpec((nq, gw, t), lambda b, g, i: (b, g, 0)),
                  pl.BlockSpec((1, 1, gw), lambda b, g, i: (b, 0, g)),
                  pl.BlockSpec((t, t), lambda b, g, i: (0, 0))],
        out_specs=pl.BlockSpec((t, gw), lambda b, g, i: (b * nq + i, g)),
        out_shape=jax.ShapeDtypeStruct((n, width), BF16),
        scratch_shapes=[pltpu.VMEM(shape, F32) for shape in ((1, t), (2 * HEAD_DIM, t)) for _ in range(HEAD_GROUP)],
        compiler_params=_cparams("arbitrary", "arbitrary", "arbitrary"),
        name="stick_breaking_attention",
    )(qt, k.reshape(bsz, seq, width), vt, kabs, tri)


def _in1_kernel(x_ref, sc_ref, sh_ref, gn_ref, wk_ref, wqt_ref, wvt_ref, qt_ref, k_ref, vt_ref, kabs_ref,
                *, tiles_per_seq):
    i = pl.program_id(0)
    tm = x_ref.shape[0]
    h = _adaln(x_ref[...], gn_ref[...], sc_ref[0], sh_ref[0]).astype(BF16)
    k = _dot(h, wk_ref[...]).astype(BF16)
    k_ref[...] = k
    qt = lax.dot_general(wqt_ref[...], h, NT_DIMS, preferred_element_type=F32)
    qt_ref[...] = (qt * (HEAD_DIM ** -0.5)).astype(BF16)
    vt = lax.dot_general(wvt_ref[...], h, NT_DIMS, preferred_element_type=F32).astype(BF16)
    for blk in range(tm // ATTN_TILE):
        vt_ref[blk] = vt[:, blk * ATTN_TILE:(blk + 1) * ATTN_TILE]
    kabs = jnp.max(jnp.abs(k.astype(F32)), axis=0, keepdims=True)

    @pl.when(i % tiles_per_seq == 0)
    def _first():
        kabs_ref[0] = kabs

    @pl.when(i % tiles_per_seq != 0)
    def _rest():
        kabs_ref[0] = jnp.maximum(kabs_ref[0], kabs)


def _in_proj1(x2d, seq, sc, sh, gain, w_in):
    n, d = x2d.shape
    tm = TOKEN_TILE
    tiles_per_seq = seq // tm
    width = w_in.shape[1] // 3
    blocks = tm // ATTN_TILE
    w_bf = w_in.astype(BF16)
    row_spec = lambda w: pl.BlockSpec((tm, w), lambda i: (i, 0))
    mod_spec = pl.BlockSpec((1, 1, d), lambda i: (i // tiles_per_seq, 0, 0))
    w_spec = pl.BlockSpec((width, d), lambda i: (0, 0))
    return pl.pallas_call(
        functools.partial(_in1_kernel, tiles_per_seq=tiles_per_seq),
        grid=(n // tm,),
        in_specs=[row_spec(d), mod_spec, mod_spec, pl.BlockSpec((1, d), lambda i: (0, 0)),
                  pl.BlockSpec((d, width), lambda i: (0, 0)), w_spec, w_spec],
        out_specs=[pl.BlockSpec((width, tm), lambda i: (0, i)), row_spec(width),
                   pl.BlockSpec((blocks, width, ATTN_TILE), lambda i: (i, 0, 0)),
                   pl.BlockSpec((1, 1, width), lambda i: (i // tiles_per_seq, 0, 0))],
        out_shape=[jax.ShapeDtypeStruct((width, n), BF16), jax.ShapeDtypeStruct((n, width), BF16),
                   jax.ShapeDtypeStruct((n // ATTN_TILE, width, ATTN_TILE), BF16),
                   jax.ShapeDtypeStruct((n // seq, 1, width), F32)],
        compiler_params=_cparams("arbitrary"),
        name="layer1_in_proj",
    )(x2d, sc, sh, gain.reshape(1, d), w_bf[:, width:2 * width], w_bf[:, :width].T, w_bf[:, 2 * width:].T)


def _out_kernel(*refs, widths):
    n_in = len(widths)
    y_refs = refs[:n_in]
    w_ref, x_ref, g1_ref, gn_ref, sc_ref, sh_ref, x1_ref, h2t_ref = refs[n_in:]
    y = None
    off = 0
    for y_ref, wd in zip(y_refs, widths):
        part = _dot(y_ref[...], w_ref[off:off + wd, :])
        y = part if y is None else y + part
        off += wd
    x1 = x_ref[...] + g1_ref[0] * y
    x1_ref[...] = x1
    h2 = _adaln(x1, gn_ref[...], sc_ref[0], sh_ref[0])
    h2t_ref[...] = h2.T.astype(BF16)


def _out_proj(ys, w_out, x2d, seq, g1, gain, sc, sh):
    n, d = x2d.shape
    tm = TOKEN_TILE
    tiles_per_seq = seq // tm
    widths = tuple(y.shape[1] for y in ys)
    row_spec = lambda w: pl.BlockSpec((tm, w), lambda i: (i, 0))
    mod_spec = pl.BlockSpec((1, 1, d), lambda i: (i // tiles_per_seq, 0, 0))
    return pl.pallas_call(
        functools.partial(_out_kernel, widths=widths),
        grid=(n // tm,),
        in_specs=[row_spec(w) for w in widths]
        + [pl.BlockSpec(w_out.shape, lambda i: (0, 0)), row_spec(d), mod_spec,
           pl.BlockSpec((1, d), lambda i: (0, 0)), mod_spec, mod_spec],
        out_specs=[row_spec(d), pl.BlockSpec((d, tm), lambda i: (0, i))],
        out_shape=[jax.ShapeDtypeStruct((n, d), F32), jax.ShapeDtypeStruct((d, n), BF16)],
        compiler_params=_cparams("arbitrary"),
        name="out_proj_residual",
    )(*ys, w_out.astype(BF16), x2d, g1, gain.reshape(1, d), sc, sh)


def _top_values(arrays, n):
    arrays = list(arrays)
    vals = [[] for _ in arrays]
    for _ in range(n):
        for idx, s in enumerate(arrays):
            m = jnp.max(s, axis=0, keepdims=True)
            vals[idx].append(m)
            arrays[idx] = jnp.where(s >= m, -jnp.inf, s)
    return vals


def _oddeven_merge(lo, hi, r):
    step = r * 2
    if step < hi - lo:
        yield from _oddeven_merge(lo, hi, step)
        yield from _oddeven_merge(lo + r, hi, step)
        for i in range(lo + r, hi - r, step):
            yield (i, i + r)
    else:
        yield (lo, lo + r)


def _oddeven_merge_sort(lo, hi):
    if hi - lo >= 1:
        mid = lo + (hi - lo) // 2
        yield from _oddeven_merge_sort(lo, mid)
        yield from _oddeven_merge_sort(mid + 1, hi)
        yield from _oddeven_merge(lo, hi, 1)


F32_SUBLANES = 8


def _top_sorted(s):
    n = s.shape[0] // F32_SUBLANES
    x = [s[F32_SUBLANES * i:F32_SUBLANES * (i + 1), :] for i in range(n)]

    def exchange(i, j):
        x[i], x[j] = jnp.maximum(x[i], x[j]), jnp.minimum(x[i], x[j])

    for i, j in _oddeven_merge_sort(0, n - 1):
        exchange(i, j)
    shift = F32_SUBLANES // 2
    while shift >= 1:
        y = [pltpu.roll(v, shift, 0) for v in x]
        x = [jnp.maximum(x[i], y[n - 1 - i]) for i in range(n)]
        d = n // 2
        while d >= 1:
            for i in range(n):
                if i & d == 0:
                    exchange(i, i + d)
            d //= 2
        shift //= 2
    return [v[0:1, :] for v in x]


def _peer_kernel(h2t_ref, wqt_ref, sk_ref, u_ref, vt_ref, x1_ref, g2_ref, o_ref,
                 rank_sc, p2_sc, cnt_sc, c1_sc, acc_sc):
    j = pl.program_id(1)
    n_steps = pl.num_programs(1)
    nk = PEER_NKEYS
    kk = PEER_TOPK

    @pl.when(j == 0)
    def _route():
        qt = _dot(wqt_ref[...], h2t_ref[...])
        scores = [_dot(sk_ref[h, p], qt[(2 * h + p) * nk:(2 * h + p + 1) * nk, :], precision=HIGHEST)
                  for h in range(PEER_HEADS) for p in range(2)]
        tops = [_top_sorted(sc) for sc in scores]
        assert len(tops[0]) == kk
        cands = [jnp.concatenate([tops[2 * h][r1] + tops[2 * h + 1][r2] for r1 in range(kk) for r2 in range(kk)
                                  if (r1 + 1) * (r2 + 1) <= kk], axis=0) for h in range(PEER_HEADS)]
        bests = _top_values(cands, kk)
        for h in range(PEER_HEADS):
            s1, s2 = scores[2 * h], scores[2 * h + 1]
            a1, a2 = tops[2 * h], tops[2 * h + 1]
            tau = bests[h][kk - 1]
            z = jnp.sum(jnp.where(cands[h] >= tau, jnp.exp(cands[h] - bests[h][0]), 0.0), axis=0, keepdims=True)
            top1 = jnp.concatenate(a1, axis=0)
            per_rank = jnp.zeros_like(top1)
            for r2 in range(kk):
                per_rank = per_rank + jnp.where(top1 + a2[r2] >= tau, 1.0, 0.0)
            cnt = jnp.zeros_like(s1)
            rank2 = jnp.full(s2.shape, float(kk), F32)
            for r in reversed(range(kk)):
                cnt = jnp.where(s1 >= a1[r], per_rank[r:r + 1, :], cnt)
                rank2 = jnp.where(s2 >= a2[r], float(r), rank2)
            e1 = jnp.where(s1 >= a1[kk - 1], jnp.exp(s1 - a1[0]), 0.0)
            e2 = jnp.where(s2 >= a2[kk - 1], jnp.exp(s2 - a2[0]), 0.0)
            rank_sc[h] = rank2.astype(BF16)
            p2_sc[h] = e2.astype(BF16)
            cnt_sc[h] = cnt
            c1_sc[h] = e1 / z
        acc_sc[...] = jnp.zeros_like(acc_sc)

    te = u_ref.shape[0]
    sub = PEER_EXPERT_SUBTILE
    tm = h2t_ref.shape[1]
    pack = 16
    gates = []
    for ai in range(te // nk):
        a = j * (te // nk) + ai
        w = None
        for h in range(PEER_HEADS):
            cnt = jnp.broadcast_to(cnt_sc[h, pl.ds(a, 1), :], (pack, tm)).astype(BF16)
            c1 = jnp.broadcast_to(c1_sc[h, pl.ds(a, 1), :], (pack, tm)).astype(BF16)
            rank2 = rank_sc[h].reshape(nk // pack, pack, tm)
            p2 = p2_sc[h].reshape(nk // pack, pack, tm)
            term = jnp.where(rank2 < cnt[None], p2 * c1[None], jnp.zeros((), BF16))
            w = term if w is None else w + term
        gates.append(w.reshape(nk, tm))
    h2t = h2t_ref[...]
    pre = [_dot(u_ref[si * sub:(si + 1) * sub, :], h2t) for si in range(te // sub)]
    per_sub = sub // nk
    b = jnp.concatenate(
        [jax.nn.gelu(pre[si].astype(BF16)) * jnp.concatenate(gates[si * per_sub:(si + 1) * per_sub], axis=0)
         for si in range(te // sub)], axis=0)
    acc_sc[...] += _dot(vt_ref[...], b)

    @pl.when(j == n_steps - 1)
    def _finish():
        o_ref[...] = x1_ref[...] + g2_ref[0] * acc_sc[...].T


def _peer(h2t, x1, seq, g2, layer, wq_t, subkeys, u_tab, vt_tab):
    d, n = h2t.shape
    tm = PEER_TOKEN_TILE
    te = PEER_EXPERT_TILE
    tiles_per_seq = seq // tm
    ne = u_tab.shape[1]
    heads = subkeys.shape[1]
    per_key = (heads, PEER_NKEYS, tm)
    scratch = [pltpu.VMEM(per_key, BF16), pltpu.VMEM(per_key, BF16), pltpu.VMEM(per_key, F32),
               pltpu.VMEM(per_key, F32), pltpu.VMEM((d, tm), F32)]
    return pl.pallas_call(
        _peer_kernel,
        grid=(n // tm, ne // te),
        in_specs=[
            pl.BlockSpec((d, tm), lambda i, j: (0, i)),
            pl.BlockSpec((None,) + wq_t.shape[1:], lambda i, j: (layer, 0, 0)),
            pl.BlockSpec((None,) + subkeys.shape[1:], lambda i, j: (layer, 0, 0, 0, 0)),
            pl.BlockSpec((None, te, d), lambda i, j: (layer, j, 0)),
            pl.BlockSpec((None, d, te), lambda i, j: (layer, 0, j)),
            pl.BlockSpec((tm, d), lambda i, j: (i, 0)),
            pl.BlockSpec((1, 1, d), lambda i, j: (i // tiles_per_seq, 0, 0)),
        ],
        out_specs=pl.BlockSpec((tm, d), lambda i, j: (i, 0)),
        out_shape=jax.ShapeDtypeStruct((n, d), F32),
        scratch_shapes=scratch,
        compiler_params=_cparams("arbitrary", "arbitrary"),
        name="peer_ffn",
    )(h2t, wq_t, subkeys, u_tab, vt_tab, x1, g2)


def kernel(x, c, w_mod, b_mod, norm_mix, norm_ffn, even_w_in, gmlp_v_gain, gmlp_w_s, gmlp_b_s,
           moba_q_gain, moba_k_gain, even_w_out, odd_w_in, odd_w_out, peer_w_q, peer_subkeys,
           peer_u, peer_v):
    bsz, seq, d = x.shape
    depth = w_mod.shape[0]
    assert seq % TOKEN_TILE == 0 and seq % ATTN_TILE == 0 and seq % PEER_TOKEN_TILE == 0
    mod = _modulation(c, w_mod, b_mod)
    xs = x.reshape(bsz * seq, d)
    wq_t = peer_w_q.transpose(0, 2, 1).astype(BF16)
    u_bf = peer_u.astype(BF16)
    vt_bf = peer_v.transpose(0, 2, 1).astype(BF16)
    for layer in range(depth):
        i = layer // 2
        sh1, sc1, g1, sh2, sc2, g2 = [mod[layer, :, k * d:(k + 1) * d].reshape(bsz, 1, d) for k in range(6)]
        if layer % 2 == 0:
            ya, qt, k, vt, km = _in_proj0(xs, seq, sc1, sh1, norm_mix[layer], even_w_in[i],
                                          gmlp_v_gain[i], gmlp_w_s[i], gmlp_b_s[i],
                                          moba_q_gain[i], moba_k_gain[i])
            yb = _moba(qt, k, vt, km, bsz, seq)
            ys, w_out = [ya, yb], even_w_out[i]
        else:
            qt, k, vt, kabs = _in_proj1(xs, seq, sc1, sh1, norm_mix[layer], odd_w_in[i])
            y = _stick_breaking(qt, k, vt, kabs, bsz, seq)
            ys, w_out = [y], odd_w_out[i]
        x1, h2t = _out_proj(ys, w_out, xs, seq, g1, norm_ffn[layer], sc2, sh2)
        xs = _peer(h2t, x1, seq, g2, layer, wq_t, peer_subkeys, u_bf, vt_bf)
    return xs.reshape(bsz, seq, d)
```
